```python
import math
import jax, jax.numpy as jnp
from jax import lax
import numpy as np

D_MODEL = 1024
BATCH = 4
SEQ = 8192
DEPTH = 2

CHUNK = 64
SB_HEADS = 4
SB_HEAD_DIM = 64
SB_BLOCK = 128
GDN_HEADS = 4
GDN_HEAD_DIM = 128
GDN_CONV = 4
SC_WIDTH = 256
SC_CONV = 3
D_FF = 2816
PLE_DIM = 256
LN_EPS = 1e-5
NORM_EPS = 1e-6
ALPHA = (2 * DEPTH) ** 0.25
BETA = (8 * DEPTH) ** -0.25

SB_WIDTH = SB_HEADS * SB_HEAD_DIM
GDN_WIDTH = GDN_HEADS * GDN_HEAD_DIM
MIX_WIDTH = SB_WIDTH + GDN_WIDTH + SC_WIDTH
OFF_SB = 3 * SB_WIDTH
OFF_GDN_QKV = OFF_SB + 3 * GDN_WIDTH
OFF_GDN_Z = OFF_GDN_QKV + GDN_WIDTH
OFF_GDN_A = OFF_GDN_Z + GDN_HEADS
OFF_GDN_B = OFF_GDN_A + GDN_HEADS
IN_COLS = OFF_GDN_B + 3 * SC_WIDTH

kernel_name = 'hybrid_streaming_encoder_block'


def layer_norm(x, g, b):
    xf = x.astype(jnp.float32)
    mu = jnp.mean(xf, axis=-1, keepdims=True)
    var = jnp.mean(jnp.square(xf - mu), axis=-1, keepdims=True)
    return ((xf - mu) * lax.rsqrt(var + LN_EPS) * g + b).astype(x.dtype)


def swiglu(x, w_in, w_out):
    gate, up = jnp.split(x @ w_in, 2, axis=-1)
    return (jax.nn.silu(gate) * up) @ w_out


def causal_dwconv(x, w):
    K = w.shape[0]
    S = x.shape[1]
    xp = jnp.pad(x, ((0, 0), (K - 1, 0), (0, 0)))
    return sum(xp[:, i:i + S] * w[i] for i in range(K))


def l2_normalize(t):
    tf = t.astype(jnp.float32)
    return tf * lax.rsqrt(jnp.sum(tf * tf, axis=-1, keepdims=True) + NORM_EPS)


def gated_rms_norm(o, z, w):
    of = o.astype(jnp.float32)
    y = of * lax.rsqrt(jnp.mean(of * of, axis=-1, keepdims=True) + NORM_EPS) * w
    return (y * jax.nn.silu(z.astype(jnp.float32))).astype(z.dtype)


def stick_breaking_attention(q, k, v):
    S = q.shape[2]
    scale = q.shape[-1] ** -0.5
    outs = []
    for t0 in range(0, S, SB_BLOCK):
        t1 = t0 + SB_BLOCK
        qb = q[:, :, t0:t1].astype(jnp.float32)
        kp = k[:, :, :t1].astype(jnp.float32)
        vp = v[:, :, :t1].astype(jnp.float32)
        z = jnp.einsum('bhqd,bhkd->bhqk', qb, kp) * scale
        mask = jnp.arange(t1)[None, :] < jnp.arange(t0, t1)[:, None]
        log_skip = jnp.where(mask, jax.nn.log_sigmoid(-z), 0.0)
        later = lax.cumsum(log_skip, axis=3, reverse=True) - log_skip
        att = jnp.where(mask, jnp.exp(jax.nn.log_sigmoid(z) + later), 0.0)
        outs.append(jnp.einsum('bhqk,bhkd->bhqd', att, vp))
    return jnp.concatenate(outs, axis=2).astype(v.dtype)


def gated_delta_rule_chunked(q, k, v, g, beta):
    Bsz, S, H, dk = q.shape
    dv = v.shape[-1]
    n = S // CHUNK

    def blocks(t):
        t = t.reshape(Bsz, n, CHUNK, H, *t.shape[3:])
        return jnp.swapaxes(jnp.moveaxis(t, 1, 0), 2, 3)

    q, k, v, g, beta = blocks(q), blocks(k), blocks(v), blocks(g), blocks(beta)
    gcum = jnp.cumsum(g, axis=-1)
    incl = jnp.tril(jnp.ones((CHUNK, CHUNK), dtype=bool))
    strict = jnp.tril(jnp.ones((CHUNK, CHUNK), dtype=bool), k=-1)
    decay = jnp.exp(jnp.where(incl, gcum[..., :, None] - gcum[..., None, :], -jnp.inf))
    k_beta = k * beta[..., None]
    m = jnp.where(strict, jnp.einsum('nbhik,nbhjk->nbhij', k_beta, k) * decay, 0.0)
    eye = jnp.eye(CHUNK, dtype=m.dtype)
    t_inv = lax.linalg.triangular_solve(eye + m, jnp.broadcast_to(eye, m.shape),
                                        left_side=True, lower=True)
    u = jnp.einsum('nbhij,nbhjv->nbhiv', t_inv, v * beta[..., None])
    w = jnp.einsum('nbhij,nbhjk->nbhik', t_inv, k_beta * jnp.exp(gcum)[..., None])
    qk = jnp.einsum('nbhik,nbhjk->nbhij', q, k) * decay
    q_dec = q * jnp.exp(gcum)[..., None]
    k_dec = k * jnp.exp(gcum[..., -1:] - gcum)[..., None]
    g_last = jnp.exp(gcum[..., -1])

    def step(state, xs):
        u_c, w_c, qk_c, q_c, k_c, gl = xs
        v_new = u_c - jnp.einsum('bhik,bhkv->bhiv', w_c, state)
        o = jnp.einsum('bhik,bhkv->bhiv', q_c, state) + jnp.einsum('bhij,bhjv->bhiv', qk_c, v_new)
        state = state * gl[..., None, None] + jnp.einsum('bhik,bhiv->bhkv', k_c, v_new)
        return state, o

    s0 = jnp.zeros((Bsz, H, dk, dv), jnp.float32)
    _, o = lax.scan(step, s0, (u, w, qk, q_dec, k_dec, g_last))
    return jnp.moveaxis(jnp.swapaxes(o, 2, 3), 0, 1).reshape(Bsz, S, H, dv)


def token_mix(h, w_in, gdn_conv_w, gdn_a_log, gdn_dt_bias, gdn_norm_w, sc_conv_w, w_out):
    Bsz, S, _ = h.shape
    proj = h @ w_in
    sb_qkv, gdn_qkv, gdn_z, gdn_a, gdn_b, sc_bch = jnp.split(
        proj, [OFF_SB, OFF_GDN_QKV, OFF_GDN_Z, OFF_GDN_A, OFF_GDN_B], axis=-1)

    sb_q, sb_k, sb_v = [t.reshape(Bsz, S, SB_HEADS, SB_HEAD_DIM).transpose(0, 2, 1, 3)
                        for t in jnp.split(sb_qkv, 3, axis=-1)]
    o_sb = stick_breaking_attention(sb_q, sb_k, sb_v).transpose(0, 2, 1, 3).reshape(Bsz, S, SB_WIDTH)

    gdn_qkv = jax.nn.silu(causal_dwconv(gdn_qkv, gdn_conv_w))
    g_q, g_k, g_v = [t.reshape(Bsz, S, GDN_HEADS, GDN_HEAD_DIM) for t in jnp.split(gdn_qkv, 3, axis=-1)]
    g_q = l2_normalize(g_q) * (GDN_HEAD_DIM ** -0.5)
    g_k = l2_normalize(g_k)
    beta = jax.nn.sigmoid(gdn_b.astype(jnp.float32))
    log_decay = -jnp.exp(gdn_a_log.astype(jnp.float32)) * jax.nn.softplus(
        gdn_a.astype(jnp.float32) + gdn_dt_bias.astype(jnp.float32))
    o_gdn = gated_delta_rule_chunked(g_q, g_k, g_v.astype(jnp.float32), log_decay, beta)
    o_gdn = gated_rms_norm(o_gdn, gdn_z.reshape(Bsz, S, GDN_HEADS, GDN_HEAD_DIM), gdn_norm_w)
    o_gdn = o_gdn.reshape(Bsz, S, GDN_WIDTH)

    sc_b, sc_c, sc_h = jnp.split(sc_bch, 3, axis=-1)
    o_sc = sc_b * causal_dwconv(sc_c * sc_h, sc_conv_w)

    mixed = jnp.concatenate([o_sb.astype(h.dtype), o_gdn.astype(h.dtype), o_sc], axis=-1)
    return mixed @ w_out


def setup_inputs(seed: int = 0) -> dict:
    key = jax.random.key(seed)
    ks = jax.random.split(key, 16)

    def nrm(k, shape, scale):
        return scale * jax.random.normal(k, shape, jnp.float32)

    dt = jnp.exp(jax.random.uniform(ks[9], (DEPTH, GDN_HEADS), jnp.float32,
                                    math.log(1e-3), math.log(1e-1)))
    return {
        'x': nrm(ks[0], (BATCH, SEQ, D_MODEL), 1.0),
        'p': nrm(ks[1], (DEPTH, BATCH, SEQ, PLE_DIM), 1.0),
        'ln_g': 1.0 + nrm(ks[2], (DEPTH, 4, D_MODEL), 0.02),
        'ln_b': nrm(ks[3], (DEPTH, 4, D_MODEL), 0.02),
        'ffn_w_in': nrm(ks[4], (DEPTH, 2, D_MODEL, 2 * D_FF), D_MODEL ** -0.5),
        'ffn_w_out': nrm(ks[5], (DEPTH, 2, D_FF, D_MODEL), BETA * D_FF ** -0.5),
        'mix_w_in': nrm(ks[6], (DEPTH, D_MODEL, IN_COLS), D_MODEL ** -0.5),
        'gdn_conv_w': nrm(ks[7], (DEPTH, GDN_CONV, 3 * GDN_WIDTH), GDN_CONV ** -0.5),
        'gdn_a_log': jnp.log(jax.random.uniform(ks[8], (DEPTH, GDN_HEADS), jnp.float32, 1.0, 16.0)),
        'gdn_dt_bias': jnp.log(jnp.expm1(dt)),
        'gdn_norm_w': 1.0 + nrm(ks[10], (DEPTH, GDN_HEAD_DIM), 0.02),
        'sc_conv_w': nrm(ks[11], (DEPTH, SC_CONV, SC_WIDTH), SC_CONV ** -0.5),
        'mix_w_out': nrm(ks[12], (DEPTH, MIX_WIDTH, D_MODEL), BETA * MIX_WIDTH ** -0.5),
        'ple_w_proj': nrm(ks[13], (DEPTH, PLE_DIM, D_MODEL), BETA * PLE_DIM ** -0.5),
        'ple_w_gate': nrm(ks[14], (DEPTH, D_MODEL, D_MODEL), D_MODEL ** -0.5),
        'ple_b_gate': nrm(ks[15], (DEPTH, D_MODEL), 0.02),
    }


def reference(x, p, ln_g, ln_b, ffn_w_in, ffn_w_out, mix_w_in, gdn_conv_w, gdn_a_log,
              gdn_dt_bias, gdn_norm_w, sc_conv_w, mix_w_out, ple_w_proj, ple_w_gate, ple_b_gate):
    for i in range(DEPTH):
        x = layer_norm(ALPHA * x + 0.5 * swiglu(x, ffn_w_in[i, 0], ffn_w_out[i, 0]), ln_g[i, 0], ln_b[i, 0])
        mix = token_mix(x, mix_w_in[i], gdn_conv_w[i], gdn_a_log[i], gdn_dt_bias[i],
                        gdn_norm_w[i], sc_conv_w[i], mix_w_out[i])
        x = layer_norm(ALPHA * x + mix, ln_g[i, 1], ln_b[i, 1])
        x = layer_norm(ALPHA * x + 0.5 * swiglu(x, ffn_w_in[i, 1], ffn_w_out[i, 1]), ln_g[i, 2], ln_b[i, 2])
        ple = jax.nn.sigmoid(x @ ple_w_gate[i] + ple_b_gate[i]) * (p[i] @ ple_w_proj[i])
        x = layer_norm(ALPHA * x + ple, ln_g[i, 3], ln_b[i, 3])
    return x
```

```python
import functools

import jax
import jax.numpy as jnp
from jax import lax
from jax.experimental import pallas as pl
from jax.experimental.pallas import tpu as pltpu

F32 = jnp.float32
BF16 = jnp.bfloat16

D_MODEL = 1024
DEPTH = 2
CHUNK = 64
SB_HEADS = 4
SB_HEAD_DIM = 64
GDN_HEADS = 4
GDN_HEAD_DIM = 128
GDN_CONV = 4
SC_WIDTH = 256
SC_CONV = 3
D_FF = 2816
PLE_DIM = 256
LN_EPS = 1e-5
NORM_EPS = 1e-6
ALPHA = (2 * DEPTH) ** 0.25

SB_WIDTH = SB_HEADS * SB_HEAD_DIM
GDN_WIDTH = GDN_HEADS * GDN_HEAD_DIM
OFF_SB = 3 * SB_WIDTH
OFF_GDN_QKV = OFF_SB + 3 * GDN_WIDTH
OFF_GDN_Z = OFF_GDN_QKV + GDN_WIDTH
OFF_GDN_A = OFF_GDN_Z + GDN_HEADS
OFF_GDN_B = OFF_GDN_A + GDN_HEADS
IN_COLS = OFF_GDN_B + 3 * SC_WIDTH

LANES = 128
WIDE = GDN_HEADS * CHUNK
FF_CHUNK = 256
ROW_TILE = 512
SB_TILE = 256
GDN_ROWS = 512
HALO = 8
SB_SKIP_LOG = 100.0
VMEM_LIMIT = 56 * 1024 * 1024


def _layer_norm(y, g, b):
    mu = jnp.mean(y, axis=-1, keepdims=True)
    d = y - mu
    var = jnp.mean(d * d, axis=-1, keepdims=True)
    return d * lax.rsqrt(var + LN_EPS) * g + b


def _softplus(x):
    return jnp.maximum(x, 0.0) + jnp.log(1.0 + jnp.exp(-jnp.abs(x)))


def _dot(a, b):
    return jnp.dot(a, b, preferred_element_type=F32)


def _dot_nt(a, b):
    return lax.dot_general(a, b, (((1,), (1,)), ((), ())), preferred_element_type=F32)


def _dot_tn(a, b):
    return lax.dot_general(a, b, (((0,), (0,)), ((), ())), preferred_element_type=F32)


def _resident(shape):
    nd = len(shape)
    return pl.BlockSpec(shape, lambda *_: (0,) * nd, pipeline_mode=pl.Buffered(1))


def _params(*sem):
    return pltpu.CompilerParams(dimension_semantics=sem, vmem_limit_bytes=VMEM_LIMIT)


def _ffn_body(x_ref, win_ref, wout_ref, g_ref, b_ref, o_ref, acc_ref):
    x = x_ref[...]
    xb = x.astype(BF16)
    for c in range(D_FF // FF_CHUNK):
        lo = c * FF_CHUNK
        gate = _dot(xb, win_ref[:, lo:lo + FF_CHUNK])
        up = _dot(xb, win_ref[:, D_FF + lo:D_FF + lo + FF_CHUNK])
        h = (gate * jax.nn.sigmoid(gate) * up).astype(BF16)
        part = _dot(h, wout_ref[lo:lo + FF_CHUNK, :])
        if c == 0:
            acc_ref[...] = part
        else:
            acc_ref[...] += part
    o_ref[...] = _layer_norm(ALPHA * x + 0.5 * acc_ref[...], g_ref[...], b_ref[...])


def _ffn_ln(x, w_in, w_out, g, b):
    n, d = x.shape
    tm = min(ROW_TILE, n)
    row = pl.BlockSpec((tm, d), lambda i: (i, 0))
    return pl.pallas_call(
        _ffn_body,
        grid=(n // tm,),
        in_specs=[row, _resident(w_in.shape), _resident(w_out.shape),
                  _resident(g.shape), _resident(b.shape)],
        out_specs=row,
        out_shape=jax.ShapeDtypeStruct((n, d), F32),
        scratch_shapes=[pltpu.VMEM((tm, d), F32)],
        compiler_params=_params("parallel"),
        name="ffn_ln",
    )(x, w_in, w_out, g, b)


_PROJ_GROUPS = (("sb", 3 * SB_WIDTH, BF16), ("gqkv", 3 * GDN_WIDTH, F32), ("gz", GDN_WIDTH, F32),
                ("ga", WIDE, F32), ("gb", WIDE, F32), ("sc", 3 * SC_WIDTH, F32))
_PROJ_COLS = sum(w for _, w, _ in _PROJ_GROUPS)


def _proj_body(h_ref, w_ref, *out_refs):
    hb = h_ref[...].astype(BF16)
    lo = 0
    for (_, width, dtype), o_ref in zip(_PROJ_GROUPS, out_refs):
        o_ref[...] = _dot(hb, w_ref[:, lo:lo + width]).astype(dtype)
        lo += width


def _proj(h, w):
    n, d = h.shape
    tm = min(ROW_TILE, n)
    return pl.pallas_call(
        _proj_body,
        grid=(n // tm,),
        in_specs=[pl.BlockSpec((tm, d), lambda i: (i, 0)), _resident(w.shape)],
        out_specs=[pl.BlockSpec((tm, width), lambda i: (i, 0)) for _, width, _ in _PROJ_GROUPS],
        out_shape=[jax.ShapeDtypeStruct((n, width), dtype) for _, width, dtype in _PROJ_GROUPS],
        compiler_params=_params("parallel"),
        name="mix_proj",
    )(h, w)


def _proj_weight(w_in):
    rep = lambda cols: jnp.repeat(cols, CHUNK, axis=1)
    return jnp.concatenate(
        [w_in[:, :OFF_GDN_Z], rep(w_in[:, OFF_GDN_Z:OFF_GDN_A]), rep(w_in[:, OFF_GDN_A:OFF_GDN_B]),
         w_in[:, OFF_GDN_B:]], axis=1).astype(BF16)


def _sb_body(q_ref, k_ref, v_ref, o_ref, *, tile):
    i = pl.program_id(2)
    scale = SB_HEAD_DIM ** -0.5
    q2 = q_ref[0]
    lane = lax.broadcasted_iota(jnp.int32, (tile, LANES), 1)
    row = lax.broadcasted_iota(jnp.int32, (tile, tile), 0)
    col = lax.broadcasted_iota(jnp.int32, (tile, tile), 1)
    causal = col < row
    suffix = (row >= col).astype(BF16)

    def block(qh, kb, carry, acc, diagonal):
        start = pl.multiple_of(kb * tile, tile)
        kblk = k_ref[0, pl.ds(start, tile), :]
        vblk = v_ref[0, pl.ds(start, tile), :]
        z = _dot_nt(qh, kblk) * scale
        log_skip = -_softplus(z)
        if diagonal:
            log_skip = jnp.where(causal, log_skip, 0.0)
        hi = log_skip.astype(BF16)
        lo = (log_skip - hi.astype(F32)).astype(BF16)
        tail = _dot(hi, suffix) + _dot(lo, suffix)
        logit = z + carry + tail
        if diagonal:
            logit = jnp.where(causal, logit, -jnp.inf)
        att = jnp.exp(logit)
        acc = acc + _dot(att.astype(BF16), vblk)
        return carry + tail[:, 0:1], acc

    out = jnp.zeros((tile, LANES), F32)
    for h in range(LANES // SB_HEAD_DIM):
        in_head = (lane >= h * SB_HEAD_DIM) & (lane < (h + 1) * SB_HEAD_DIM)
        qh = jnp.where(in_head, q2, jnp.zeros_like(q2))
        carry, acc = block(qh, i, jnp.zeros((tile, 1), F32), jnp.zeros((tile, LANES), F32), True)

        def cond(state):
            kb, carry, _ = state
            return jnp.logical_and(kb >= 0, jnp.max(carry) > -SB_SKIP_LOG)

        def body(state, qh=qh):
            kb, carry, acc = state
            carry, acc = block(qh, kb, carry, acc, False)
            return kb - 1, carry, acc

        _, _, acc = lax.while_loop(cond, body, (i - 1, carry, acc))
        out = jnp.where(in_head, acc, out)
    o_ref[0] = out.astype(o_ref.dtype)


def _sb_attention(qkv):
    bsz, s, _ = qkv.shape
    tile = min(SB_TILE, s)
    pairs = SB_WIDTH // LANES
    full = lambda off: pl.BlockSpec((1, s, LANES), lambda b, p, i: (b, 0, off + p))
    blk = pl.BlockSpec((1, tile, LANES), lambda b, p, i: (b, i, p))
    return pl.pallas_call(
        functools.partial(_sb_body, tile=tile),
        grid=(bsz, pairs, s // tile),
        in_specs=[blk, full(pairs), full(2 * pairs)],
        out_specs=blk,
        out_shape=jax.ShapeDtypeStruct((bsz, s, SB_WIDTH), BF16),
        compiler_params=_params("parallel", "parallel", "arbitrary"),
        name="sb_attention",
    )(qkv, qkv, qkv)


def _split3(x):
    hi = x.astype(BF16)
    r1 = x - hi.astype(F32)
    mid = r1.astype(BF16)
    lo = (r1 - mid.astype(F32)).astype(BF16)
    return hi, mid, lo


def _gdn_chunk(q, k, v, zgate, a_logit, b_logit, a_scale, dt_bias, norm_w, s_ref):
    c = CHUNK
    row = lax.broadcasted_iota(jnp.int32, (c, WIDE), 0)
    lane = lax.broadcasted_iota(jnp.int32, (c, WIDE), 1)
    j = lane & (c - 1)
    eye = row == j
    incl = row >= j
    strict = row > j

    def block_diag(x, width):
        tiled = jnp.concatenate([x] * GDN_HEADS, axis=0)
        r = lax.broadcasted_iota(jnp.int32, tiled.shape, 0) // c
        l = lax.broadcasted_iota(jnp.int32, tiled.shape, 1) // width
        return jnp.where(r == l, tiled, jnp.zeros_like(tiled))

    def to_row(x_col):
        return jnp.sum(jnp.where(eye, x_col, 0.0), axis=0, keepdims=True)

    g = a_scale * _softplus(a_logit + dt_bias)
    beta = jax.nn.sigmoid(b_logit)
    tri = (lax.broadcasted_iota(jnp.int32, (c, c), 0) >= lax.broadcasted_iota(jnp.int32, (c, c), 1)).astype(BF16)
    gcum = sum(_dot(tri, part) for part in _split3(g))
    grow = to_row(gcum)
    brow = to_row(beta)
    decay = jnp.exp(jnp.where(incl, gcum - grow, -jnp.inf))

    kb = k.astype(BF16)
    k_bd = block_diag(kb, GDN_HEAD_DIM)
    kk = _dot_nt(kb, k_bd)
    m = jnp.where(strict, kk * beta * decay, 0.0)

    x = jnp.where(eye, 1.0, 0.0) - jnp.where((row ^ j) == 1, m, 0.0)
    size = 2
    while size < c:
        joined = ((row ^ j) >= size) & ((row ^ j) < 2 * size)
        cmat = jnp.where(joined, m, 0.0).astype(BF16)
        xb = x.astype(BF16)
        y = _dot(cmat, block_diag(xb, c))
        x = x - _dot(xb, block_diag(y.astype(BF16), c))
        size *= 2

    qk = jnp.where(incl, _dot_nt(q.astype(BF16), k_bd) * decay, 0.0)
    t_u = x * brow
    t_w = t_u * jnp.exp(grow)
    u = _dot(t_u.astype(BF16), block_diag(v.astype(BF16), GDN_HEAD_DIM))
    w = _dot(t_w.astype(BF16), k_bd)

    heads = range(GDN_HEADS)
    col = lambda x, h: x[:, h * GDN_HEAD_DIM:(h + 1) * GDN_HEAD_DIM]
    states = [s_ref[h] for h in heads]
    sbf = [s.astype(BF16) for s in states]
    v_new = [col(u, h) - _dot(col(w, h).astype(BF16), sbf[h]) for h in heads]
    intra = _dot(qk.astype(BF16), block_diag(jnp.concatenate(v_new, axis=1).astype(BF16), GDN_HEAD_DIM))
    outs = []
    for h in heads:
        gcol = jnp.broadcast_to(gcum[:, h * c:h * c + 1], (c, GDN_HEAD_DIM))
        glast = gcum[c - 1:c, h * c:h * c + 1]
        o = jnp.exp(gcol) * _dot(col(q, h).astype(BF16), sbf[h]) + col(intra, h)
        k_dec_v = (v_new[h] * jnp.exp(glast - gcol)).astype(BF16)
        s_ref[h] = states[h] * jnp.exp(glast) + _dot_tn(col(kb, h), k_dec_v)
        y = o * lax.rsqrt(jnp.mean(o * o, axis=-1, keepdims=True) + NORM_EPS) * norm_w
        zh = col(zgate, h)
        outs.append(y * (zh * jax.nn.sigmoid(zh)))
    return jnp.concatenate(outs, axis=1)


def _gdn_body(x_ref, z_ref, a_ref, b_ref, cw_ref, ascale_ref, dtb_ref, nw_ref, o_ref,
              hist_ref, qkv_ref, s_ref, *, rows):
    step = pl.program_id(1)

    @pl.when(step == 0)
    def _():
        hist_ref[0:HALO, :] = jnp.zeros((HALO, 3 * GDN_WIDTH), F32)
        s_ref[...] = jnp.zeros_like(s_ref)

    @pl.when(step != 0)
    def _():
        hist_ref[0:HALO, :] = hist_ref[rows:rows + HALO, :]

    hist_ref[HALO:HALO + rows, :] = x_ref[0]
    conv = sum(cw_ref[t:t + 1, :] * hist_ref[HALO - (GDN_CONV - 1) + t:HALO - (GDN_CONV - 1) + t + rows, :]
               for t in range(GDN_CONV))
    act = conv * jax.nn.sigmoid(conv)
    for h in range(2 * GDN_HEADS):
        lo = h * GDN_HEAD_DIM
        t = act[:, lo:lo + GDN_HEAD_DIM]
        n = t * lax.rsqrt(jnp.sum(t * t, axis=-1, keepdims=True) + NORM_EPS)
        qkv_ref[:, lo:lo + GDN_HEAD_DIM] = n * (GDN_HEAD_DIM ** -0.5) if h < GDN_HEADS else n
    qkv_ref[:, 2 * GDN_WIDTH:] = act[:, 2 * GDN_WIDTH:]

    a_scale = ascale_ref[...]
    dt_bias = dtb_ref[...]
    norm_w = nw_ref[...]

    def chunk(ci, _):
        r = pl.ds(pl.multiple_of(ci * CHUNK, CHUNK), CHUNK)
        o = _gdn_chunk(qkv_ref[r, 0:GDN_WIDTH], qkv_ref[r, GDN_WIDTH:2 * GDN_WIDTH],
                       qkv_ref[r, 2 * GDN_WIDTH:3 * GDN_WIDTH], z_ref[0, r, :], a_ref[0, r, :], b_ref[0, r, :],
                       a_scale, dt_bias, norm_w, s_ref)
        o_ref[0, r, :] = o.astype(o_ref.dtype)
        return 0

    lax.fori_loop(0, rows // CHUNK, chunk, 0)


def _gdn(gqkv, gz, ga, gb, conv_w, a_log, dt_bias, norm_w):
    bsz, s, _ = gqkv.shape
    rows = min(GDN_ROWS, s)
    a_scale = jnp.repeat(-jnp.exp(a_log.astype(F32)), CHUNK)[None, :]
    dtb = jnp.repeat(dt_bias.astype(F32), CHUNK)[None, :]
    blk = lambda width: pl.BlockSpec((1, rows, width), lambda b, i: (b, i, 0))
    return pl.pallas_call(
        functools.partial(_gdn_body, rows=rows),
        grid=(bsz, s // rows),
        in_specs=[blk(3 * GDN_WIDTH), blk(GDN_WIDTH), blk(WIDE), blk(WIDE),
                  _resident(conv_w.shape), _resident(a_scale.shape), _resident(dtb.shape),
                  _resident((1, GDN_HEAD_DIM))],
        out_specs=blk(GDN_WIDTH),
        out_shape=jax.ShapeDtypeStruct((bsz, s, GDN_WIDTH), BF16),
        scratch_shapes=[pltpu.VMEM((rows + HALO, 3 * GDN_WIDTH), F32),
                        pltpu.VMEM((rows, 3 * GDN_WIDTH), F32),
                        pltpu.VMEM((GDN_HEADS, GDN_HEAD_DIM, GDN_HEAD_DIM), F32)],
        compiler_params=_params("parallel", "arbitrary"),
        name="gated_deltanet",
    )(gqkv, gz, ga, gb, conv_w, a_scale, dtb, norm_w[None, :])


def _mix_out_body(x_ref, osb_ref, ogdn_ref, sc_ref, cw_ref, w_ref, g_ref, b_ref, o_ref, hist_ref, *, rows):
    step = pl.program_id(1)
    sc = sc_ref[0]
    ch = sc[:, SC_WIDTH:2 * SC_WIDTH] * sc[:, 2 * SC_WIDTH:]

    @pl.when(step == 0)
    def _():
        hist_ref[0:HALO, :] = jnp.zeros((HALO, SC_WIDTH), F32)

    @pl.when(step != 0)
    def _():
        hist_ref[0:HALO, :] = hist_ref[rows:rows + HALO, :]

    hist_ref[HALO:HALO + rows, :] = ch
    conv = sum(cw_ref[t:t + 1, :] * hist_ref[HALO - (SC_CONV - 1) + t:HALO - (SC_CONV - 1) + t + rows, :]
               for t in range(SC_CONV))
    o_sc = (sc[:, 0:SC_WIDTH] * conv).astype(BF16)
    mix = (_dot(osb_ref[0], w_ref[0:SB_WIDTH, :])
           + _dot(ogdn_ref[0], w_ref[SB_WIDTH:SB_WIDTH + GDN_WIDTH, :])
           + _dot(o_sc, w_ref[SB_WIDTH + GDN_WIDTH:, :]))
    o_ref[0] = _layer_norm(ALPHA * x_ref[0] + mix, g_ref[...], b_ref[...])


def _mix_out(x, o_sb, o_gdn, sc, conv_w, w_out, g, b):
    bsz, s, d = x.shape
    rows = min(ROW_TILE, s)
    blk = lambda width: pl.BlockSpec((1, rows, width), lambda bi, i: (bi, i, 0))
    return pl.pallas_call(
        functools.partial(_mix_out_body, rows=rows),
        grid=(bsz, s // rows),
        in_specs=[blk(d), blk(SB_WIDTH), blk(GDN_WIDTH), blk(3 * SC_WIDTH),
                  _resident(conv_w.shape), _resident(w_out.shape), _resident(g.shape), _resident(b.shape)],
        out_specs=blk(d),
        out_shape=jax.ShapeDtypeStruct((bsz, s, d), F32),
        scratch_shapes=[pltpu.VMEM((rows + HALO, SC_WIDTH), F32)],
        compiler_params=_params("parallel", "arbitrary"),
        name="mix_out_ln",
    )(x, o_sb, o_gdn, sc, conv_w, w_out, g, b)


def _ple_body(x_ref, p_ref, wg_ref, bg_ref, wp_ref, g_ref, b_ref, o_ref):
    x = x_ref[...]
    gate = jax.nn.sigmoid(_dot(x.astype(BF16), wg_ref[...]) + bg_ref[...])
    emb = _dot(p_ref[...].astype(BF16), wp_ref[...])
    o_ref[...] = _layer_norm(ALPHA * x + gate * emb, g_ref[...], b_ref[...])


def _ple_ln(x, p, w_gate, b_gate, w_proj, g, b):
    n, d = x.shape
    tm = min(ROW_TILE, n)
    row = lambda width: pl.BlockSpec((tm, width), lambda i: (i, 0))
    return pl.pallas_call(
        _ple_body,
        grid=(n // tm,),
        in_specs=[row(d), row(p.shape[1]), _resident(w_gate.shape), _resident(b_gate.shape),
                  _resident(w_proj.shape), _resident(g.shape), _resident(b.shape)],
        out_specs=row(d),
        out_shape=jax.ShapeDtypeStruct((n, d), F32),
        compiler_params=_params("parallel"),
        name="ple_ln",
    )(x, p, w_gate, b_gate, w_proj, g, b)


def _token_mix_ln(x, w_in, gdn_conv_w, gdn_a_log, gdn_dt_bias, gdn_norm_w, sc_conv_w, w_out, g, b):
    bsz, s, d = x.shape
    sb, gqkv, gz, ga, gb, sc = _proj(x.reshape(bsz * s, d), _proj_weight(w_in))
    shaped = lambda t: t.reshape(bsz, s, t.shape[-1])
    o_sb = _sb_attention(shaped(sb))
    o_gdn = _gdn(shaped(gqkv), shaped(gz), shaped(ga), shaped(gb), gdn_conv_w, gdn_a_log, gdn_dt_bias, gdn_norm_w)
    return _mix_out(x, o_sb, o_gdn, shaped(sc), sc_conv_w, w_out.astype(BF16), g, b)


def kernel(x, p, ln_g, ln_b, ffn_w_in, ffn_w_out, mix_w_in, gdn_conv_w, gdn_a_log, gdn_dt_bias, gdn_norm_w,
           sc_conv_w, mix_w_out, ple_w_proj, ple_w_gate, ple_b_gate):
    bsz, s, d = x.shape
    n = bsz * s
    gain = lambda i, j: ln_g[i, j][None, :]
    bias = lambda i, j: ln_b[i, j][None, :]
    for i in range(ln_g.shape[0]):
        x = _ffn_ln(x.reshape(n, d), ffn_w_in[i, 0].astype(BF16), ffn_w_out[i, 0].astype(BF16), gain(i, 0), bias(i, 0))
        x = _token_mix_ln(x.reshape(bsz, s, d), mix_w_in[i], gdn_conv_w[i], gdn_a_log[i], gdn_dt_bias[i],
                          gdn_norm_w[i], sc_conv_w[i], mix_w_out[i], gain(i, 1), bias(i, 1))
        x = _ffn_ln(x.reshape(n, d), ffn_w_in[i, 1].astype(BF16), ffn_w_out[i, 1].astype(BF16), gain(i, 2), bias(i, 2))
        x = _ple_ln(x, p[i].reshape(n, -1), ple_w_gate[i].astype(BF16), ple_b_gate[i][None, :],
                    ple_w_proj[i].astype(BF16), gain(i, 3), bias(i, 3))
    return x.reshape(bsz, s, d)
```

```python
import functools

import jax
import jax.numpy as jnp
from jax import lax
from jax.experimental import pallas as pl
from jax.experimental.pallas import tpu as pltpu

F32 = jnp.float32
BF16 = jnp.bfloat16

D_MODEL = 1024
DEPTH = 2
CHUNK = 64
SB_HEADS = 4
SB_HEAD_DIM = 64
GDN_HEADS = 4
GDN_HEAD_DIM = 128
GDN_CONV = 4
SC_WIDTH = 256
SC_CONV = 3
D_FF = 2816
PLE_DIM = 256
LN_EPS = 1e-5
NORM_EPS = 1e-6
ALPHA = (2 * DEPTH) ** 0.25

SB_WIDTH = SB_HEADS * SB_HEAD_DIM
GDN_WIDTH = GDN_HEADS * GDN_HEAD_DIM
END_SB = 3 * SB_WIDTH
END_GDN_QKV = END_SB + 3 * GDN_WIDTH
END_GDN_Z = END_GDN_QKV + GDN_WIDTH
END_GDN_A = END_GDN_Z + GDN_HEADS
END_GDN_B = END_GDN_A + GDN_HEADS
IN_COLS = END_GDN_B + 3 * SC_WIDTH

LANES = 128
WIDE = GDN_HEADS * CHUNK
FF_CHUNK = 256
ROW_TILE = 512
SB_ROWS = 512
SB_TILE = 128
GDN_ROWS = 256
HALO = 8
SB_SKIP_LOG = 100.0
VMEM_LIMIT = 56 * 1024 * 1024


def _layer_norm(y, g, b):
    mu = jnp.mean(y, axis=-1, keepdims=True)
    d = y - mu
    var = jnp.mean(d * d, axis=-1, keepdims=True)
    return d * lax.rsqrt(var + LN_EPS) * g + b


def _softplus(x):
    return jnp.maximum(x, 0.0) + jnp.log(1.0 + jnp.exp(-jnp.abs(x)))


def _dot(a, b):
    return jnp.dot(a, b, preferred_element_type=F32)


def _dot_nt(a, b):
    return lax.dot_general(a, b, (((1,), (1,)), ((), ())), preferred_element_type=F32)


def _dot_tn(a, b):
    return lax.dot_general(a, b, (((0,), (0,)), ((), ())), preferred_element_type=F32)


def _head_cols(x, h):
    return x[:, h * GDN_HEAD_DIM:(h + 1) * GDN_HEAD_DIM]


def _resident(shape):
    nd = len(shape)
    return pl.BlockSpec(shape, lambda *_: (0,) * nd, pipeline_mode=pl.Buffered(1))


def _params(*sem):
    return pltpu.CompilerParams(dimension_semantics=sem, vmem_limit_bytes=VMEM_LIMIT)


def _ffn_body(x_ref, win_ref, wout_ref, g_ref, b_ref, o_ref, acc_ref):
    x = x_ref[...]
    xb = x.astype(BF16)
    for c in range(D_FF // FF_CHUNK):
        lo = c * FF_CHUNK
        gate = _dot(xb, win_ref[:, lo:lo + FF_CHUNK])
        up = _dot(xb, win_ref[:, D_FF + lo:D_FF + lo + FF_CHUNK])
        h = (gate * jax.nn.sigmoid(gate) * up).astype(BF16)
        part = _dot(h, wout_ref[lo:lo + FF_CHUNK, :])
        if c == 0:
            acc_ref[...] = part
        else:
            acc_ref[...] += part
    o_ref[...] = _layer_norm(ALPHA * x + 0.5 * acc_ref[...], g_ref[...], b_ref[...])


def _ffn_ln(x, w_in, w_out, g, b):
    n, d = x.shape
    tm = min(ROW_TILE, n)
    row = pl.BlockSpec((tm, d), lambda i: (i, 0))
    return pl.pallas_call(
        _ffn_body,
        grid=(n // tm,),
        in_specs=[row, _resident(w_in.shape), _resident(w_out.shape),
                  _resident(g.shape), _resident(b.shape)],
        out_specs=row,
        out_shape=jax.ShapeDtypeStruct((n, d), F32),
        scratch_shapes=[pltpu.VMEM((tm, d), F32)],
        compiler_params=_params("parallel"),
        name="ffn_ln",
    )(x, w_in, w_out, g, b)


def _causal_conv(hist_ref, w_ref, rows):
    taps = w_ref.shape[0]
    xe = hist_ref[...]
    return sum(w_ref[taps - 1 - d:taps - d, :] * (xe if d == 0 else pltpu.roll(xe, d, 0))[HALO:HALO + rows, :]
               for d in range(taps))


def _proj_body(h_ref, w_ref, gcw_ref, scw_ref, sb_ref, gqkv_ref, gz_ref, ga_ref, gb_ref, osc_ref,
               ghist_ref, shist_ref, *, rows, tiles_per_seq):
    first = lax.rem(pl.program_id(0), tiles_per_seq) == 0

    @pl.when(first)
    def _():
        ghist_ref[0:HALO, :] = jnp.zeros((HALO, 3 * GDN_WIDTH), F32)
        shist_ref[0:HALO, :] = jnp.zeros((HALO, SC_WIDTH), F32)

    @pl.when(jnp.logical_not(first))
    def _():
        ghist_ref[0:HALO, :] = ghist_ref[rows:rows + HALO, :]
        shist_ref[0:HALO, :] = shist_ref[rows:rows + HALO, :]

    hb = h_ref[...].astype(BF16)
    lo = 0

    def project(width):
        nonlocal lo
        out = _dot(hb, w_ref[:, lo:lo + width])
        lo += width
        return out

    sb_ref[...] = project(3 * SB_WIDTH).astype(BF16)
    ghist_ref[HALO:HALO + rows, :] = project(3 * GDN_WIDTH)
    gz_ref[...] = project(GDN_WIDTH)
    ga_ref[...] = project(WIDE)
    gb_ref[...] = project(WIDE)
    sc = project(3 * SC_WIDTH)

    shist_ref[HALO:HALO + rows, :] = sc[:, SC_WIDTH:2 * SC_WIDTH] * sc[:, 2 * SC_WIDTH:]
    osc_ref[...] = (sc[:, 0:SC_WIDTH] * _causal_conv(shist_ref, scw_ref, rows)).astype(BF16)

    conv = _causal_conv(ghist_ref, gcw_ref, rows)
    act = conv * jax.nn.sigmoid(conv)
    for h in range(2 * GDN_HEADS):
        t = _head_cols(act, h)
        n = t * lax.rsqrt(jnp.sum(t * t, axis=-1, keepdims=True) + NORM_EPS)
        if h < GDN_HEADS:
            n = n * GDN_HEAD_DIM ** -0.5
        gqkv_ref[:, h * GDN_HEAD_DIM:(h + 1) * GDN_HEAD_DIM] = n.astype(BF16)
    gqkv_ref[:, 2 * GDN_WIDTH:] = act[:, 2 * GDN_WIDTH:].astype(BF16)


_PROJ_OUTS = ((3 * SB_WIDTH, BF16), (3 * GDN_WIDTH, BF16), (GDN_WIDTH, F32), (WIDE, F32), (WIDE, F32), (SC_WIDTH, BF16))


def _proj(h, w, gdn_conv_w, sc_conv_w, seq):
    n, d = h.shape
    tm = min(ROW_TILE, seq)
    return pl.pallas_call(
        functools.partial(_proj_body, rows=tm, tiles_per_seq=seq // tm),
        grid=(n // tm,),
        in_specs=[pl.BlockSpec((tm, d), lambda i: (i, 0)), _resident(w.shape), _resident(gdn_conv_w.shape),
                  _resident(sc_conv_w.shape)],
        out_specs=[pl.BlockSpec((tm, width), lambda i: (i, 0)) for width, _ in _PROJ_OUTS],
        out_shape=[jax.ShapeDtypeStruct((n, width), dtype) for width, dtype in _PROJ_OUTS],
        scratch_shapes=[pltpu.VMEM((tm + HALO, 3 * GDN_WIDTH), F32), pltpu.VMEM((tm + HALO, SC_WIDTH), F32)],
        compiler_params=_params("arbitrary"),
        name="mix_proj",
    )(h, w, gdn_conv_w, sc_conv_w)


def _proj_weight(w_in):
    rep = lambda cols: jnp.repeat(cols, CHUNK, axis=1)
    return jnp.concatenate(
        [w_in[:, :END_GDN_Z], rep(w_in[:, END_GDN_Z:END_GDN_A]), rep(w_in[:, END_GDN_A:END_GDN_B]),
         w_in[:, END_GDN_B:]], axis=1).astype(BF16)


def _sb_body(q_ref, k_ref, v_ref, o_ref, acc_ref, suf_ref, *, rows, tile):
    i = pl.program_id(2)
    subtiles = rows // tile
    heads = LANES // SB_HEAD_DIM
    win = 2 * tile
    lane = lax.broadcasted_iota(jnp.int32, (tile, LANES), 1)

    @pl.when(i == 0)
    def _():
        j = lax.broadcasted_iota(jnp.int32, (2 * win, win), 0) & (win - 1)
        s = lax.broadcasted_iota(jnp.int32, (2 * win, win), 1)
        suf_ref[...] = (j >= s).astype(BF16)

    def scores(qh, start, width, lead):
        z = _dot_nt(qh, k_ref[0, pl.ds(start, width), :])
        log_skip = -_softplus(z)
        seen = None
        if lead is not None:
            seen = (lax.broadcasted_iota(jnp.int32, (tile, width), 1)
                    - lax.broadcasted_iota(jnp.int32, (tile, width), 0)) < lead
            log_skip = jnp.where(seen, log_skip, 0.0)
        hi = log_skip.astype(BF16)
        lo = (log_skip - hi.astype(F32)).astype(BF16)
        if width == win:
            op = suf_ref[...]
        else:
            op = jnp.concatenate([suf_ref[0:width, 0:width], suf_ref[win:win + width, 0:width]], axis=0)
        tail = _dot(jnp.concatenate([hi, lo], axis=1), op)
        return z, tail, seen

    def weights(z, tail, seen, carry):
        logit = z + tail if carry is None else z + carry + tail
        if seen is not None:
            logit = jnp.where(seen, logit, -jnp.inf)
        total = tail[:, 0:1]
        return jnp.exp(logit).astype(BF16), total if carry is None else carry + total

    def values(att, start, width):
        return _dot(att, v_ref[0, pl.ds(start, width), :])

    setup = []
    for t in range(subtiles):
        q2 = q_ref[0, t * tile:(t + 1) * tile, :] * SB_HEAD_DIM ** -0.5
        diag = i * subtiles + t
        start = pl.multiple_of(jnp.maximum(diag - 1, 0) * tile, tile)
        for h in range(heads):
            in_head = (lane >= h * SB_HEAD_DIM) & (lane < (h + 1) * SB_HEAD_DIM)
            setup.append((jnp.where(in_head, q2, jnp.zeros_like(q2)), diag, start))
    sc = [scores(qh, start, win, diag * tile - start) for qh, diag, start in setup]
    wt = [weights(z, tail, seen, None) for z, tail, seen in sc]
    problems = [(qh, diag - 2, wt[n][1], values(wt[n][0], start, win)) for n, (qh, diag, start) in enumerate(setup)]

    worst = functools.reduce(jnp.maximum, [carry for _, _, carry, _ in problems])
    for n, (_, _, _, acc) in enumerate(problems):
        acc_ref[n] = acc

    @pl.when(jnp.logical_and(i * subtiles + subtiles - 1 >= 2, jnp.max(worst) > -SB_SKIP_LOG))
    def _():
        for n, (qh, first, carry, acc) in enumerate(problems):
            def cond(state):
                kb, carry, _ = state
                return jnp.logical_and(kb >= 0, jnp.max(carry) > -SB_SKIP_LOG)

            def body(state, qh=qh):
                kb, carry, acc = state
                start = pl.multiple_of(kb * tile, tile)
                z, tail, _ = scores(qh, start, tile, None)
                att, carry = weights(z, tail, None, carry)
                return kb - 1, carry, acc + values(att, start, tile)

            acc_ref[n] = lax.while_loop(cond, body, (first, carry, acc))[2]

    for t in range(subtiles):
        out = acc_ref[t * heads]
        for h in range(1, heads):
            out = jnp.where(lane >= h * SB_HEAD_DIM, acc_ref[t * heads + h], out)
        o_ref[0, t * tile:(t + 1) * tile, :] = out.astype(o_ref.dtype)


def _sb_attention(qkv):
    bsz, s, _ = qkv.shape
    rows = min(SB_ROWS, s)
    tile = min(SB_TILE, rows // 2)
    pairs = SB_WIDTH // LANES
    full = lambda off: pl.BlockSpec((1, s, LANES), lambda b, p, i: (b, 0, off + p))
    blk = pl.BlockSpec((1, rows, LANES), lambda b, p, i: (b, i, p))
    problems = (rows // tile) * (LANES // SB_HEAD_DIM)
    return pl.pallas_call(
        functools.partial(_sb_body, rows=rows, tile=tile),
        grid=(bsz, pairs, s // rows),
        in_specs=[blk, full(pairs), full(2 * pairs)],
        out_specs=blk,
        out_shape=jax.ShapeDtypeStruct((bsz, s, SB_WIDTH), BF16),
        scratch_shapes=[pltpu.VMEM((problems, tile, LANES), F32), pltpu.VMEM((4 * tile, 2 * tile), BF16)],
        compiler_params=_params("parallel", "parallel", "arbitrary"),
        name="sb_attention",
    )(qkv, qkv, qkv)


def _split3(x):
    hi = x.astype(BF16)
    r1 = x - hi.astype(F32)
    mid = r1.astype(BF16)
    lo = (r1 - mid.astype(F32)).astype(BF16)
    return hi, mid, lo


def _block_diag(x, width):
    tiled = jnp.concatenate([x] * GDN_HEADS, axis=0)
    r = lax.broadcasted_iota(jnp.int32, tiled.shape, 0) // CHUNK
    l = lax.broadcasted_iota(jnp.int32, tiled.shape, 1) // width
    return jnp.where(r == l, tiled, jnp.zeros_like(tiled))


def _gdn_intra(qs, ks, vs, a_logits, b_logits, a_scale, dt_bias):
    c = CHUNK
    units = range(len(qs))
    row = lax.broadcasted_iota(jnp.int32, (c, WIDE), 0)
    lane = lax.broadcasted_iota(jnp.int32, (c, WIDE), 1)
    j = lane & (c - 1)
    eye = row == j
    incl = row >= j
    strict = row > j

    def to_row(x_col):
        return jnp.sum(jnp.where(eye, x_col, 0.0), axis=0, keepdims=True)

    tri = (lax.broadcasted_iota(jnp.int32, (c, c), 0) >= lax.broadcasted_iota(jnp.int32, (c, c), 1)).astype(BF16)
    g = [a_scale * _softplus(a + dt_bias) for a in a_logits]
    beta = [jax.nn.sigmoid(b) for b in b_logits]
    gcum = [sum(_dot(tri, part) for part in _split3(gi)) for gi in g]
    grow = [to_row(t) for t in gcum]
    brow = [to_row(t) for t in beta]
    decay = [jnp.exp(jnp.where(incl, gcum[n] - grow[n], -jnp.inf)) for n in units]

    k_bd = [_block_diag(k, GDN_HEAD_DIM) for k in ks]
    kk = [_dot_nt(ks[n], k_bd[n]) for n in units]
    qk = [_dot_nt(qs[n], k_bd[n]) for n in units]
    m = [jnp.where(strict, kk[n] * beta[n] * decay[n], 0.0) for n in units]

    x = [jnp.where(eye, 1.0, 0.0) - jnp.where((row ^ j) == 1, mi, 0.0) for mi in m]
    size = 2
    while size < c:
        joined = ((row ^ j) >= size) & ((row ^ j) < 2 * size)
        xb = [xi.astype(BF16) for xi in x]
        y = [_dot(jnp.where(joined, m[n], 0.0).astype(BF16), _block_diag(xb[n], c)) for n in units]
        z = [_dot(xb[n], _block_diag(y[n].astype(BF16), c)) for n in units]
        x = [x[n] - z[n] for n in units]
        size *= 2

    t_u = [x[n] * brow[n] for n in units]
    t_w = [t_u[n] * jnp.exp(grow[n]) for n in units]
    u = [_dot(t_u[n].astype(BF16), _block_diag(vs[n], GDN_HEAD_DIM)) for n in units]
    w = [_dot(t_w[n].astype(BF16), k_bd[n]) for n in units]

    spread = lambda t: jnp.concatenate(
        [jnp.broadcast_to(t[:, h * c:h * c + 1], (c, GDN_HEAD_DIM)) for h in range(GDN_HEADS)], axis=1)
    return [(u[n], w[n].astype(BF16), jnp.where(incl, qk[n] * decay[n], 0.0).astype(BF16),
             spread(jnp.exp(gcum[n])), spread(jnp.exp(gcum[n][c - 1:c, :] - gcum[n]))) for n in units]


def _gdn_body(x_ref, z_ref, a_ref, b_ref, ascale_ref, dtb_ref, nw_ref, o_ref,
              u_ref, w_ref, qk_ref, eg_ref, ed_ref, oraw_ref, s_ref, *, bsz, rows):
    chunks = rows // CHUNK
    heads = range(GDN_HEADS)
    batch = range(bsz)
    qkv_ref = x_ref

    @pl.when(pl.program_id(0) == 0)
    def _():
        s_ref[...] = jnp.zeros_like(s_ref)

    a_scale = ascale_ref[...]
    dt_bias = dtb_ref[...]

    units = [(b, pl.ds(ci * CHUNK, CHUNK)) for ci in range(chunks) for b in batch]
    results = _gdn_intra(
        [qkv_ref[b, r, 0:GDN_WIDTH] for b, r in units], [qkv_ref[b, r, GDN_WIDTH:2 * GDN_WIDTH] for b, r in units],
        [qkv_ref[b, r, 2 * GDN_WIDTH:] for b, r in units], [a_ref[b, r, :] for b, r in units],
        [b_ref[b, r, :] for b, r in units], a_scale, dt_bias)
    for (b, r), (u, w, qk, eg, ed) in zip(units, results):
        u_ref[b, r, :] = u
        w_ref[b, r, :] = w
        qk_ref[b, r, :] = qk
        eg_ref[b, r, :] = eg
        ed_ref[b, r, :] = ed

    def recur(ci, _):
        r = pl.ds(pl.multiple_of(ci * CHUNK, CHUNK), CHUNK)
        last = pl.ds(ci * CHUNK + CHUNK - 1, 1)
        states = [[s_ref[b, h] for h in heads] for b in batch]
        sbf = [[s.astype(BF16) for s in row] for row in states]
        v_new = [[_head_cols(u_ref[b, r, :], h) - _dot(_head_cols(w_ref[b, r, :], h), sbf[b][h]) for h in heads]
                 for b in batch]
        carried = [jnp.concatenate([_dot(_head_cols(qkv_ref[b, r, 0:GDN_WIDTH], h), sbf[b][h]) for h in heads], axis=1)
                   for b in batch]
        local = [_dot(qk_ref[b, r, :], _block_diag(jnp.concatenate(v_new[b], axis=1).astype(BF16), GDN_HEAD_DIM))
                 for b in batch]
        for b in batch:
            k = qkv_ref[b, r, GDN_WIDTH:2 * GDN_WIDTH]
            ed = ed_ref[b, r, :]
            eg_last = eg_ref[b, last, :]
            for h in heads:
                k_dec_v = (v_new[b][h] * _head_cols(ed, h)).astype(BF16)
                s_ref[b, h] = states[b][h] * _head_cols(eg_last, h) + _dot_tn(_head_cols(k, h), k_dec_v)
        for b in batch:
            oraw_ref[b, r, :] = eg_ref[b, r, :] * carried[b] + local[b]
        return 0

    lax.fori_loop(0, chunks, recur, 0)

    norm_w = nw_ref[...]
    for b in batch:
        for h in heads:
            o = _head_cols(oraw_ref[b], h)
            y = o * lax.rsqrt(jnp.mean(o * o, axis=-1, keepdims=True) + NORM_EPS) * norm_w
            zh = _head_cols(z_ref[b], h)
            o_ref[b, :, h * GDN_HEAD_DIM:(h + 1) * GDN_HEAD_DIM] = (y * (zh * jax.nn.sigmoid(zh))).astype(o_ref.dtype)


def _gdn(gqkv, gz, ga, gb, a_log, dt_bias, norm_w):
    bsz, s, _ = gqkv.shape
    rows = min(GDN_ROWS, s)
    a_scale = jnp.repeat(-jnp.exp(a_log.astype(F32)), CHUNK)[None, :]
    dtb = jnp.repeat(dt_bias.astype(F32), CHUNK)[None, :]
    blk = lambda width: pl.BlockSpec((bsz, rows, width), lambda i: (0, i, 0))
    buf = lambda width, dtype: pltpu.VMEM((bsz, rows, width), dtype)
    return pl.pallas_call(
        functools.partial(_gdn_body, bsz=bsz, rows=rows),
        grid=(s // rows,),
        in_specs=[blk(3 * GDN_WIDTH), blk(GDN_WIDTH), blk(WIDE), blk(WIDE),
                  _resident(a_scale.shape), _resident(dtb.shape), _resident((1, GDN_HEAD_DIM))],
        out_specs=blk(GDN_WIDTH),
        out_shape=jax.ShapeDtypeStruct((bsz, s, GDN_WIDTH), BF16),
        scratch_shapes=[buf(GDN_WIDTH, F32), buf(GDN_WIDTH, BF16), buf(WIDE, BF16),
                        buf(GDN_WIDTH, F32), buf(GDN_WIDTH, F32), buf(GDN_WIDTH, F32),
                        pltpu.VMEM((bsz, GDN_HEADS, GDN_HEAD_DIM, GDN_HEAD_DIM), F32)],
        compiler_params=_params("arbitrary"),
        name="gated_deltanet",
    )(gqkv, gz, ga, gb, a_scale, dtb, norm_w[None, :])


def _mix_out_body(x_ref, osb_ref, ogdn_ref, osc_ref, w_ref, g_ref, b_ref, o_ref):
    mix = (_dot(osb_ref[...], w_ref[0:SB_WIDTH, :])
           + _dot(ogdn_ref[...], w_ref[SB_WIDTH:SB_WIDTH + GDN_WIDTH, :])
           + _dot(osc_ref[...], w_ref[SB_WIDTH + GDN_WIDTH:, :]))
    o_ref[...] = _layer_norm(ALPHA * x_ref[...] + mix, g_ref[...], b_ref[...])


def _mix_out(x, o_sb, o_gdn, o_sc, w_out, g, b):
    n, d = x.shape
    tm = min(ROW_TILE, n)
    row = lambda width: pl.BlockSpec((tm, width), lambda i: (i, 0))
    return pl.pallas_call(
        _mix_out_body,
        grid=(n // tm,),
        in_specs=[row(d), row(SB_WIDTH), row(GDN_WIDTH), row(SC_WIDTH),
                  _resident(w_out.shape), _resident(g.shape), _resident(b.shape)],
        out_specs=row(d),
        out_shape=jax.ShapeDtypeStruct((n, d), F32),
        compiler_params=_params("parallel"),
        name="mix_out_ln",
    )(x, o_sb, o_gdn, o_sc, w_out, g, b)


def _ple_body(x_ref, p_ref, wg_ref, bg_ref, wp_ref, g_ref, b_ref, o_ref):
    x = x_ref[...]
    gate = jax.nn.sigmoid(_dot(x.astype(BF16), wg_ref[...]) + bg_ref[...])
    emb = _dot(p_ref[...].astype(BF16), wp_ref[...])
    o_ref[...] = _layer_norm(ALPHA * x + gate * emb, g_ref[...], b_ref[...])


def _ple_ln(x, p, w_gate, b_gate, w_proj, g, b):
    n, d = x.shape
    tm = min(ROW_TILE, n)
    row = lambda width: pl.BlockSpec((tm, width), lambda i: (i, 0))
    return pl.pallas_call(
        _ple_body,
        grid=(n // tm,),
        in_specs=[row(d), row(p.shape[1]), _resident(w_gate.shape), _resident(b_gate.shape),
                  _resident(w_proj.shape), _resident(g.shape), _resident(b.shape)],
        out_specs=row(d),
        out_shape=jax.ShapeDtypeStruct((n, d), F32),
        compiler_params=_params("parallel"),
        name="ple_ln",
    )(x, p, w_gate, b_gate, w_proj, g, b)


def _token_mix_ln(x, w_in, gdn_conv_w, gdn_a_log, gdn_dt_bias, gdn_norm_w, sc_conv_w, w_out, g, b):
    bsz, s, d = x.shape
    flat = x.reshape(bsz * s, d)
    sb, gqkv, gz, ga, gb, o_sc = _proj(flat, _proj_weight(w_in), gdn_conv_w, sc_conv_w, s)
    shaped = lambda t: t.reshape(bsz, s, t.shape[-1])
    o_sb = _sb_attention(shaped(sb))
    o_gdn = _gdn(shaped(gqkv), shaped(gz), shaped(ga), shaped(gb), gdn_a_log, gdn_dt_bias, gdn_norm_w)
    out = _mix_out(flat, o_sb.reshape(bsz * s, -1), o_gdn.reshape(bsz * s, -1), o_sc, w_out.astype(BF16), g, b)
    return out.reshape(bsz, s, d)


def kernel(x, p, ln_g, ln_b, ffn_w_in, ffn_w_out, mix_w_in, gdn_conv_w, gdn_a_log, gdn_dt_bias, gdn_norm_w,
           sc_conv_w, mix_w_out, ple_w_proj, ple_w_gate, ple_b_gate):
    bsz, s, d = x.shape
    n = bsz * s
    gain = lambda i, j: ln_g[i, j][None, :]
    bias = lambda i, j: ln_b[i, j][None, :]
    for i in range(ln_g.shape[0]):
        x = _ffn_ln(x.reshape(n, d), ffn_w_in[i, 0].astype(BF16), ffn_w_out[i, 0].astype(BF16), gain(i, 0), bias(i, 0))
        x = _token_mix_ln(x.reshape(bsz, s, d), mix_w_in[i], gdn_conv_w[i], gdn_a_log[i], gdn_dt_bias[i],
                          gdn_norm_w[i], sc_conv_w[i], mix_w_out[i], gain(i, 1), bias(i, 1))
        x = _ffn_ln(x.reshape(n, d), ffn_w_in[i, 1].astype(BF16), ffn_w_out[i, 1].astype(BF16), gain(i, 2), bias(i, 2))
        x = _ple_ln(x, p[i].reshape(n, -1), ple_w_gate[i].astype(BF16), ple_b_gate[i][None, :],
                    ple_w_proj[i].astype(BF16), gain(i, 3), bias(i, 3))
    return x.reshape(bsz, s, d)
```

```python
import functools

import jax
import jax.numpy as jnp
from jax import lax
from jax.experimental import pallas as pl
from jax.experimental.pallas import tpu as pltpu

F32 = jnp.float32
BF16 = jnp.bfloat16

D_MODEL = 1024
DEPTH = 2
CHUNK = 64
SB_HEADS = 4
SB_HEAD_DIM = 64
GDN_HEADS = 4
GDN_HEAD_DIM = 128
GDN_CONV = 4
SC_WIDTH = 256
SC_CONV = 3
D_FF = 2816
PLE_DIM = 256
LN_EPS = 1e-5
NORM_EPS = 1e-6
ALPHA = (2 * DEPTH) ** 0.25

SB_WIDTH = SB_HEADS * SB_HEAD_DIM
GDN_WIDTH = GDN_HEADS * GDN_HEAD_DIM
END_SB = 3 * SB_WIDTH
END_GDN_QKV = END_SB + 3 * GDN_WIDTH
END_GDN_Z = END_GDN_QKV + GDN_WIDTH
END_GDN_A = END_GDN_Z + GDN_HEADS
END_GDN_B = END_GDN_A + GDN_HEADS
IN_COLS = END_GDN_B + 3 * SC_WIDTH

LANES = 128
WIDE = GDN_HEADS * CHUNK
FF_CHUNK = 256
ROW_TILE = 512
SB_ROWS = 512
SB_TILE = 128
SB_WINDOW_TILES = 3
GDN_ROWS = 256
HALO = 8
SB_SKIP_LOG = 100.0
VMEM_LIMIT = 56 * 1024 * 1024


def _layer_norm(y, g, b):
    mu = jnp.mean(y, axis=-1, keepdims=True)
    d = y - mu
    var = jnp.mean(d * d, axis=-1, keepdims=True)
    return d * lax.rsqrt(var + LN_EPS) * g + b


def _softplus(x):
    return jnp.maximum(x, 0.0) + jnp.log(1.0 + jnp.exp(-jnp.abs(x)))


def _dot(a, b):
    return jnp.dot(a, b, preferred_element_type=F32)


def _dot_nt(a, b):
    return lax.dot_general(a, b, (((1,), (1,)), ((), ())), preferred_element_type=F32)


def _dot_tn(a, b):
    return lax.dot_general(a, b, (((0,), (0,)), ((), ())), preferred_element_type=F32)


def _head_cols(x, h):
    return x[:, h * GDN_HEAD_DIM:(h + 1) * GDN_HEAD_DIM]


def _resident(shape):
    nd = len(shape)
    return pl.BlockSpec(shape, lambda *_: (0,) * nd, pipeline_mode=pl.Buffered(1))


def _params(*sem):
    return pltpu.CompilerParams(dimension_semantics=sem, vmem_limit_bytes=VMEM_LIMIT)


def _swiglu(x, win_ref, wout_ref, acc_ref):
    xb = x.astype(BF16)
    for c in range(D_FF // FF_CHUNK):
        lo = c * FF_CHUNK
        gate = _dot(xb, win_ref[:, lo:lo + FF_CHUNK])
        up = _dot(xb, win_ref[:, D_FF + lo:D_FF + lo + FF_CHUNK])
        h = (gate * jax.nn.sigmoid(gate) * up).astype(BF16)
        part = _dot(h, wout_ref[lo:lo + FF_CHUNK, :])
        if c == 0:
            acc_ref[...] = part
        else:
            acc_ref[...] += part
    return acc_ref[...]


def _ffn_body(x_ref, win_ref, wout_ref, g_ref, b_ref, o_ref, acc_ref):
    x = x_ref[...]
    o_ref[...] = _layer_norm(ALPHA * x + 0.5 * _swiglu(x, win_ref, wout_ref, acc_ref), g_ref[...], b_ref[...])


def _ffn_ln(x, w_in, w_out, g, b):
    n, d = x.shape
    tm = min(ROW_TILE, n)
    row = pl.BlockSpec((tm, d), lambda i: (i, 0))
    return pl.pallas_call(
        _ffn_body,
        grid=(n // tm,),
        in_specs=[row, _resident(w_in.shape), _resident(w_out.shape),
                  _resident(g.shape), _resident(b.shape)],
        out_specs=row,
        out_shape=jax.ShapeDtypeStruct((n, d), F32),
        scratch_shapes=[pltpu.VMEM((tm, d), F32)],
        compiler_params=_params("parallel"),
        name="ffn_ln",
    )(x, w_in, w_out, g, b)


def _causal_conv(hist_ref, w_ref, rows):
    taps = w_ref.shape[0]
    xe = hist_ref[...]
    return sum(w_ref[taps - 1 - d:taps - d, :] * (xe if d == 0 else pltpu.roll(xe, d, 0))[HALO:HALO + rows, :]
               for d in range(taps))


def _proj_body(h_ref, w_ref, gcw_ref, scw_ref, sb_ref, gqkv_ref, gz_ref, ga_ref, gb_ref, osc_ref,
               ghist_ref, shist_ref, *, rows, tiles_per_seq):
    first = lax.rem(pl.program_id(0), tiles_per_seq) == 0

    @pl.when(first)
    def _():
        ghist_ref[0:HALO, :] = jnp.zeros((HALO, 3 * GDN_WIDTH), F32)
        shist_ref[0:HALO, :] = jnp.zeros((HALO, SC_WIDTH), F32)

    @pl.when(jnp.logical_not(first))
    def _():
        ghist_ref[0:HALO, :] = ghist_ref[rows:rows + HALO, :]
        shist_ref[0:HALO, :] = shist_ref[rows:rows + HALO, :]

    hb = h_ref[...].astype(BF16)
    lo = 0

    def project(width):
        nonlocal lo
        out = _dot(hb, w_ref[:, lo:lo + width])
        lo += width
        return out

    sb_ref[...] = project(3 * SB_WIDTH).astype(BF16)
    ghist_ref[HALO:HALO + rows, :] = project(3 * GDN_WIDTH)
    gz_ref[...] = project(GDN_WIDTH)
    ga_ref[...] = project(WIDE)
    gb_ref[...] = project(WIDE)
    sc = project(3 * SC_WIDTH)

    shist_ref[HALO:HALO + rows, :] = sc[:, SC_WIDTH:2 * SC_WIDTH] * sc[:, 2 * SC_WIDTH:]
    osc_ref[...] = (sc[:, 0:SC_WIDTH] * _causal_conv(shist_ref, scw_ref, rows)).astype(BF16)

    conv = _causal_conv(ghist_ref, gcw_ref, rows)
    act = conv * jax.nn.sigmoid(conv)
    for h in range(2 * GDN_HEADS):
        t = _head_cols(act, h)
        n = t * lax.rsqrt(jnp.sum(t * t, axis=-1, keepdims=True) + NORM_EPS)
        if h < GDN_HEADS:
            n = n * GDN_HEAD_DIM ** -0.5
        gqkv_ref[:, h * GDN_HEAD_DIM:(h + 1) * GDN_HEAD_DIM] = n.astype(BF16)
    gqkv_ref[:, 2 * GDN_WIDTH:] = act[:, 2 * GDN_WIDTH:].astype(BF16)


_PROJ_OUTS = ((3 * SB_WIDTH, BF16), (3 * GDN_WIDTH, BF16), (GDN_WIDTH, F32), (WIDE, F32), (WIDE, F32), (SC_WIDTH, BF16))


def _proj(h, w, gdn_conv_w, sc_conv_w, seq):
    n, d = h.shape
    tm = min(ROW_TILE, seq)
    return pl.pallas_call(
        functools.partial(_proj_body, rows=tm, tiles_per_seq=seq // tm),
        grid=(n // tm,),
        in_specs=[pl.BlockSpec((tm, d), lambda i: (i, 0)), _resident(w.shape), _resident(gdn_conv_w.shape),
                  _resident(sc_conv_w.shape)],
        out_specs=[pl.BlockSpec((tm, width), lambda i: (i, 0)) for width, _ in _PROJ_OUTS],
        out_shape=[jax.ShapeDtypeStruct((n, width), dtype) for width, dtype in _PROJ_OUTS],
        scratch_shapes=[pltpu.VMEM((tm + HALO, 3 * GDN_WIDTH), F32), pltpu.VMEM((tm + HALO, SC_WIDTH), F32)],
        compiler_params=_params("arbitrary"),
        name="mix_proj",
    )(h, w, gdn_conv_w, sc_conv_w)


def _proj_weight(w_in):
    rep = lambda cols: jnp.repeat(cols, CHUNK, axis=1)
    return jnp.concatenate(
        [w_in[:, :END_GDN_Z], rep(w_in[:, END_GDN_Z:END_GDN_A]), rep(w_in[:, END_GDN_A:END_GDN_B]),
         w_in[:, END_GDN_B:]], axis=1).astype(BF16)


def _sb_body(q_ref, k_ref, v_ref, o_ref, acc_ref, suf_ref, *, rows, tile):
    i = pl.program_id(2)
    subtiles = rows // tile
    heads = LANES // SB_HEAD_DIM
    win = SB_WINDOW_TILES * tile
    lane = lax.broadcasted_iota(jnp.int32, (tile, LANES), 1)

    @pl.when(i == 0)
    def _():
        j = lax.broadcasted_iota(jnp.int32, (2 * win, win), 0)
        j = jnp.where(j >= win, j - win, j)
        s = lax.broadcasted_iota(jnp.int32, (2 * win, win), 1)
        suf_ref[...] = (j >= s).astype(BF16)

    def scores(qh, start, width, lead):
        z = _dot_nt(qh, k_ref[0, pl.ds(start, width), :])
        log_skip = -_softplus(z)
        seen = None
        if lead is not None:
            seen = (lax.broadcasted_iota(jnp.int32, (tile, width), 1)
                    - lax.broadcasted_iota(jnp.int32, (tile, width), 0)) < lead
            log_skip = jnp.where(seen, log_skip, 0.0)
        hi = log_skip.astype(BF16)
        lo = (log_skip - hi.astype(F32)).astype(BF16)
        if width == win:
            op = suf_ref[...]
        else:
            op = jnp.concatenate([suf_ref[0:width, 0:width], suf_ref[win:win + width, 0:width]], axis=0)
        tail = _dot(jnp.concatenate([hi, lo], axis=1), op)
        return z, tail, seen

    def weights(z, tail, seen, carry):
        logit = z + tail if carry is None else z + carry + tail
        if seen is not None:
            logit = jnp.where(seen, logit, -jnp.inf)
        total = tail[:, 0:1]
        return jnp.exp(logit).astype(BF16), total if carry is None else carry + total

    def values(att, start, width):
        return _dot(att, v_ref[0, pl.ds(start, width), :])

    setup = []
    for t in range(subtiles):
        q2 = q_ref[0, t * tile:(t + 1) * tile, :] * SB_HEAD_DIM ** -0.5
        diag = i * subtiles + t
        start = pl.multiple_of(jnp.maximum(diag - (SB_WINDOW_TILES - 1), 0) * tile, tile)
        for h in range(heads):
            in_head = (lane >= h * SB_HEAD_DIM) & (lane < (h + 1) * SB_HEAD_DIM)
            setup.append((jnp.where(in_head, q2, jnp.zeros_like(q2)), diag, start))
    sc = [scores(qh, start, win, diag * tile - start) for qh, diag, start in setup]
    wt = [weights(z, tail, seen, None) for z, tail, seen in sc]
    problems = [(qh, diag - SB_WINDOW_TILES, wt[n][1], values(wt[n][0], start, win))
                for n, (qh, diag, start) in enumerate(setup)]

    worst = functools.reduce(jnp.maximum, [carry for _, _, carry, _ in problems])
    for n, (_, _, _, acc) in enumerate(problems):
        acc_ref[n] = acc

    @pl.when(jnp.logical_and(i * subtiles + subtiles - 1 >= SB_WINDOW_TILES, jnp.max(worst) > -SB_SKIP_LOG))
    def _():
        for n, (qh, first, carry, acc) in enumerate(problems):
            def cond(state):
                kb, carry, _ = state
                return jnp.logical_and(kb >= 0, jnp.max(carry) > -SB_SKIP_LOG)

            def body(state, qh=qh):
                kb, carry, acc = state
                start = pl.multiple_of(kb * tile, tile)
                z, tail, _ = scores(qh, start, tile, None)
                att, carry = weights(z, tail, None, carry)
                return kb - 1, carry, acc + values(att, start, tile)

            acc_ref[n] = lax.while_loop(cond, body, (first, carry, acc))[2]

    for t in range(subtiles):
        out = acc_ref[t * heads]
        for h in range(1, heads):
            out = jnp.where(lane >= h * SB_HEAD_DIM, acc_ref[t * heads + h], out)
        o_ref[0, t * tile:(t + 1) * tile, :] = out.astype(o_ref.dtype)


def _sb_attention(qkv):
    bsz, s, _ = qkv.shape
    rows = min(SB_ROWS, s)
    tile = min(SB_TILE, rows // 2)
    assert s >= SB_WINDOW_TILES * tile
    pairs = SB_WIDTH // LANES
    full = lambda off: pl.BlockSpec((1, s, LANES), lambda b, p, i: (b, 0, off + p))
    blk = pl.BlockSpec((1, rows, LANES), lambda b, p, i: (b, i, p))
    problems = (rows // tile) * (LANES // SB_HEAD_DIM)
    return pl.pallas_call(
        functools.partial(_sb_body, rows=rows, tile=tile),
        grid=(bsz, pairs, s // rows),
        in_specs=[blk, full(pairs), full(2 * pairs)],
        out_specs=blk,
        out_shape=jax.ShapeDtypeStruct((bsz, s, SB_WIDTH), BF16),
        scratch_shapes=[pltpu.VMEM((problems, tile, LANES), F32),
                        pltpu.VMEM((2 * SB_WINDOW_TILES * tile, SB_WINDOW_TILES * tile), BF16)],
        compiler_params=_params("parallel", "parallel", "arbitrary"),
        name="sb_attention",
    )(qkv, qkv, qkv)


def _split3(x):
    hi = x.astype(BF16)
    r1 = x - hi.astype(F32)
    mid = r1.astype(BF16)
    lo = (r1 - mid.astype(F32)).astype(BF16)
    return hi, mid, lo


def _block_diag(x, width):
    tiled = jnp.concatenate([x] * GDN_HEADS, axis=0)
    r = lax.broadcasted_iota(jnp.int32, tiled.shape, 0) // CHUNK
    l = lax.broadcasted_iota(jnp.int32, tiled.shape, 1) // width
    return jnp.where(r == l, tiled, jnp.zeros_like(tiled))


def _gdn_intra(qs, ks, vs, a_logits, b_logits, a_scale, dt_bias):
    c = CHUNK
    units = range(len(qs))
    row = lax.broadcasted_iota(jnp.int32, (c, WIDE), 0)
    lane = lax.broadcasted_iota(jnp.int32, (c, WIDE), 1)
    j = lane & (c - 1)
    eye = row == j
    incl = row >= j
    strict = row > j

    def to_row(x_col):
        return jnp.sum(jnp.where(eye, x_col, 0.0), axis=0, keepdims=True)

    tri = (lax.broadcasted_iota(jnp.int32, (c, c), 0) >= lax.broadcasted_iota(jnp.int32, (c, c), 1)).astype(BF16)
    g = [a_scale * _softplus(a + dt_bias) for a in a_logits]
    beta = [jax.nn.sigmoid(b) for b in b_logits]
    gcum = [sum(_dot(tri, part) for part in _split3(gi)) for gi in g]
    grow = [to_row(t) for t in gcum]
    brow = [to_row(t) for t in beta]
    decay = [jnp.exp(jnp.where(incl, gcum[n] - grow[n], -jnp.inf)) for n in units]

    k_bd = [_block_diag(k, GDN_HEAD_DIM) for k in ks]
    kk = [_dot_nt(ks[n], k_bd[n]) for n in units]
    qk = [_dot_nt(qs[n], k_bd[n]) for n in units]
    m = [jnp.where(strict, kk[n] * beta[n] * decay[n], 0.0) for n in units]

    x = [jnp.where(eye, 1.0, 0.0) - jnp.where((row ^ j) == 1, mi, 0.0) for mi in m]
    size = 2
    while size < c:
        joined = ((row ^ j) >= size) & ((row ^ j) < 2 * size)
        xb = [xi.astype(BF16) for xi in x]
        y = [_dot(jnp.where(joined, m[n], 0.0).astype(BF16), _block_diag(xb[n], c)) for n in units]
        z = [_dot(xb[n], _block_diag(y[n].astype(BF16), c)) for n in units]
        x = [x[n] - z[n] for n in units]
        size *= 2

    t_u = [x[n] * brow[n] for n in units]
    t_w = [t_u[n] * jnp.exp(grow[n]) for n in units]
    u = [_dot(t_u[n].astype(BF16), _block_diag(vs[n], GDN_HEAD_DIM)) for n in units]
    w = [_dot(t_w[n].astype(BF16), k_bd[n]) for n in units]

    spread = lambda t: jnp.concatenate(
        [jnp.broadcast_to(t[:, h * c:h * c + 1], (c, GDN_HEAD_DIM)) for h in range(GDN_HEADS)], axis=1)
    return [(u[n], w[n].astype(BF16), jnp.where(incl, qk[n] * decay[n], 0.0).astype(BF16),
             spread(jnp.exp(gcum[n])), spread(jnp.exp(gcum[n][c - 1:c, :] - gcum[n]))) for n in units]


def _gdn_body(x_ref, z_ref, a_ref, b_ref, ascale_ref, dtb_ref, nw_ref, o_ref,
              u_ref, w_ref, qk_ref, eg_ref, ed_ref, oraw_ref, s_ref, *, bsz, rows):
    chunks = rows // CHUNK
    heads = range(GDN_HEADS)
    batch = range(bsz)
    qkv_ref = x_ref

    @pl.when(pl.program_id(0) == 0)
    def _():
        s_ref[...] = jnp.zeros_like(s_ref)

    a_scale = ascale_ref[...]
    dt_bias = dtb_ref[...]

    units = [(b, pl.ds(ci * CHUNK, CHUNK)) for ci in range(chunks) for b in batch]
    results = _gdn_intra(
        [qkv_ref[b, r, 0:GDN_WIDTH] for b, r in units], [qkv_ref[b, r, GDN_WIDTH:2 * GDN_WIDTH] for b, r in units],
        [qkv_ref[b, r, 2 * GDN_WIDTH:] for b, r in units], [a_ref[b, r, :] for b, r in units],
        [b_ref[b, r, :] for b, r in units], a_scale, dt_bias)
    for (b, r), (u, w, qk, eg, ed) in zip(units, results):
        u_ref[b, r, :] = u
        w_ref[b, r, :] = w
        qk_ref[b, r, :] = qk
        eg_ref[b, r, :] = eg
        ed_ref[b, r, :] = ed

    def recur(ci, _):
        r = pl.ds(pl.multiple_of(ci * CHUNK, CHUNK), CHUNK)
        last = pl.ds(ci * CHUNK + CHUNK - 1, 1)
        states = [[s_ref[b, h] for h in heads] for b in batch]
        sbf = [[s.astype(BF16) for s in row] for row in states]
        v_new = [[_head_cols(u_ref[b, r, :], h) - _dot(_head_cols(w_ref[b, r, :], h), sbf[b][h]) for h in heads]
                 for b in batch]
        carried = [jnp.concatenate([_dot(_head_cols(qkv_ref[b, r, 0:GDN_WIDTH], h), sbf[b][h]) for h in heads], axis=1)
                   for b in batch]
        local = [_dot(qk_ref[b, r, :], _block_diag(jnp.concatenate(v_new[b], axis=1).astype(BF16), GDN_HEAD_DIM))
                 for b in batch]
        for b in batch:
            k = qkv_ref[b, r, GDN_WIDTH:2 * GDN_WIDTH]
            ed = ed_ref[b, r, :]
            eg_last = eg_ref[b, last, :]
            for h in heads:
                k_dec_v = (v_new[b][h] * _head_cols(ed, h)).astype(BF16)
                s_ref[b, h] = states[b][h] * _head_cols(eg_last, h) + _dot_tn(_head_cols(k, h), k_dec_v)
        for b in batch:
            oraw_ref[b, r, :] = eg_ref[b, r, :] * carried[b] + local[b]
        return 0

    lax.fori_loop(0, chunks, recur, 0)

    norm_w = nw_ref[...]
    for b in batch:
        for h in heads:
            o = _head_cols(oraw_ref[b], h)
            y = o * lax.rsqrt(jnp.mean(o * o, axis=-1, keepdims=True) + NORM_EPS) * norm_w
            zh = _head_cols(z_ref[b], h)
            o_ref[b, :, h * GDN_HEAD_DIM:(h + 1) * GDN_HEAD_DIM] = (y * (zh * jax.nn.sigmoid(zh))).astype(o_ref.dtype)


def _gdn(gqkv, gz, ga, gb, a_log, dt_bias, norm_w):
    bsz, s, _ = gqkv.shape
    rows = min(GDN_ROWS, s)
    a_scale = jnp.repeat(-jnp.exp(a_log.astype(F32)), CHUNK)[None, :]
    dtb = jnp.repeat(dt_bias.astype(F32), CHUNK)[None, :]
    blk = lambda width: pl.BlockSpec((bsz, rows, width), lambda i: (0, i, 0))
    buf = lambda width, dtype: pltpu.VMEM((bsz, rows, width), dtype)
    return pl.pallas_call(
        functools.partial(_gdn_body, bsz=bsz, rows=rows),
        grid=(s // rows,),
        in_specs=[blk(3 * GDN_WIDTH), blk(GDN_WIDTH), blk(WIDE), blk(WIDE),
                  _resident(a_scale.shape), _resident(dtb.shape), _resident((1, GDN_HEAD_DIM))],
        out_specs=blk(GDN_WIDTH),
        out_shape=jax.ShapeDtypeStruct((bsz, s, GDN_WIDTH), BF16),
        scratch_shapes=[buf(GDN_WIDTH, F32), buf(GDN_WIDTH, BF16), buf(WIDE, BF16),
                        buf(GDN_WIDTH, F32), buf(GDN_WIDTH, F32), buf(GDN_WIDTH, F32),
                        pltpu.VMEM((bsz, GDN_HEADS, GDN_HEAD_DIM, GDN_HEAD_DIM), F32)],
        compiler_params=_params("arbitrary"),
        name="gated_deltanet",
    )(gqkv, gz, ga, gb, a_scale, dtb, norm_w[None, :])


def _tail_body(x_ref, osb_ref, ogdn_ref, osc_ref, p_ref, wmix_ref, win_ref, wout_ref, wg_ref, bg_ref, wp_ref,
               g_ref, b_ref, o_ref, acc_ref):
    norm = lambda y, j: _layer_norm(y, g_ref[j:j + 1, :], b_ref[j:j + 1, :])
    mix = (_dot(osb_ref[...], wmix_ref[0:SB_WIDTH, :])
           + _dot(ogdn_ref[...], wmix_ref[SB_WIDTH:SB_WIDTH + GDN_WIDTH, :])
           + _dot(osc_ref[...], wmix_ref[SB_WIDTH + GDN_WIDTH:, :]))
    x1 = norm(ALPHA * x_ref[...] + mix, 1)
    x2 = norm(ALPHA * x1 + 0.5 * _swiglu(x1, win_ref, wout_ref, acc_ref), 2)
    gate = jax.nn.sigmoid(_dot(x2.astype(BF16), wg_ref[...]) + bg_ref[...])
    emb = _dot(p_ref[...].astype(BF16), wp_ref[...])
    o_ref[...] = norm(ALPHA * x2 + gate * emb, 3)


def _layer_tail(x, o_sb, o_gdn, o_sc, p, w_mix, w_in, w_out, w_gate, b_gate, w_proj, g, b):
    n, d = x.shape
    tm = min(ROW_TILE, n)
    row = lambda width: pl.BlockSpec((tm, width), lambda i: (i, 0))
    weights = (w_mix, w_in, w_out, w_gate, b_gate, w_proj, g, b)
    return pl.pallas_call(
        _tail_body,
        grid=(n // tm,),
        in_specs=[row(d), row(SB_WIDTH), row(GDN_WIDTH), row(SC_WIDTH), row(p.shape[1])]
                 + [_resident(w.shape) for w in weights],
        out_specs=row(d),
        out_shape=jax.ShapeDtypeStruct((n, d), F32),
        scratch_shapes=[pltpu.VMEM((tm, d), F32)],
        compiler_params=_params("parallel"),
        name="layer_tail",
    )(x, o_sb, o_gdn, o_sc, p, *weights)


def _token_mix(x, bsz, s, w_in, gdn_conv_w, gdn_a_log, gdn_dt_bias, gdn_norm_w, sc_conv_w):
    sb, gqkv, gz, ga, gb, o_sc = _proj(x, _proj_weight(w_in), gdn_conv_w, sc_conv_w, s)
    shaped = lambda t: t.reshape(bsz, s, t.shape[-1])
    o_sb = _sb_attention(shaped(sb))
    o_gdn = _gdn(shaped(gqkv), shaped(gz), shaped(ga), shaped(gb), gdn_a_log, gdn_dt_bias, gdn_norm_w)
    return o_sb.reshape(bsz * s, -1), o_gdn.reshape(bsz * s, -1), o_sc


def kernel(x, p, ln_g, ln_b, ffn_w_in, ffn_w_out, mix_w_in, gdn_conv_w, gdn_a_log, gdn_dt_bias, gdn_norm_w,
           sc_conv_w, mix_w_out, ple_w_proj, ple_w_gate, ple_b_gate):
    bsz, s, d = x.shape
    n = bsz * s
    x = x.reshape(n, d)
    for i in range(ln_g.shape[0]):
        x = _ffn_ln(x, ffn_w_in[i, 0].astype(BF16), ffn_w_out[i, 0].astype(BF16), ln_g[i, 0][None, :], ln_b[i, 0][None, :])
        o_sb, o_gdn, o_sc = _token_mix(x, bsz, s, mix_w_in[i], gdn_conv_w[i], gdn_a_log[i], gdn_dt_bias[i],
                                       gdn_norm_w[i], sc_conv_w[i])
        x = _layer_tail(x, o_sb, o_gdn, o_sc, p[i].reshape(n, -1), mix_w_out[i].astype(BF16),
                        ffn_w_in[i, 1].astype(BF16), ffn_w_out[i, 1].astype(BF16), ple_w_gate[i].astype(BF16),
                        ple_b_gate[i][None, :], ple_w_proj[i].astype(BF16), ln_g[i], ln_b[i])
    return x.reshape(bsz, s, d)
```

```python
import functools

import jax
import jax.numpy as jnp
from jax import lax
from jax.experimental import pallas as pl
from jax.experimental.pallas import tpu as pltpu

F32 = jnp.float32
BF16 = jnp.bfloat16

D_MODEL = 1024
DEPTH = 2
CHUNK = 64
SB_HEADS = 4
SB_HEAD_DIM = 64
GDN_HEADS = 4
GDN_HEAD_DIM = 128
GDN_CONV = 4
SC_WIDTH = 256
SC_CONV = 3
D_FF = 2816
PLE_DIM = 256
LN_EPS = 1e-5
NORM_EPS = 1e-6
ALPHA = (2 * DEPTH) ** 0.25

SB_WIDTH = SB_HEADS * SB_HEAD_DIM
GDN_WIDTH = GDN_HEADS * GDN_HEAD_DIM
END_SB = 3 * SB_WIDTH
END_GDN_QKV = END_SB + 3 * GDN_WIDTH
END_GDN_Z = END_GDN_QKV + GDN_WIDTH
END_GDN_A = END_GDN_Z + GDN_HEADS
END_GDN_B = END_GDN_A + GDN_HEADS
IN_COLS = END_GDN_B + 3 * SC_WIDTH

LANES = 128
WIDE = GDN_HEADS * CHUNK
FF_CHUNK = 256
ROW_TILE = 512
SB_ROWS = 512
SB_TILE = 128
SB_WINDOW_TILES = 3
GDN_ROWS = 256
HALO = 8
SB_SKIP_LOG = 100.0
VMEM_LIMIT = 56 * 1024 * 1024


def _layer_norm(y, g, b):
    mu = jnp.mean(y, axis=-1, keepdims=True)
    d = y - mu
    var = jnp.mean(d * d, axis=-1, keepdims=True)
    return d * lax.rsqrt(var + LN_EPS) * g + b


def _softplus(x):
    return jnp.maximum(x, 0.0) + jnp.log(1.0 + jnp.exp(-jnp.abs(x)))


def _dot(a, b):
    return jnp.dot(a, b, preferred_element_type=F32)


def _dot_nt(a, b):
    return lax.dot_general(a, b, (((1,), (1,)), ((), ())), preferred_element_type=F32)


def _dot_tn(a, b):
    return lax.dot_general(a, b, (((0,), (0,)), ((), ())), preferred_element_type=F32)


def _head_cols(x, h):
    return x[:, h * GDN_HEAD_DIM:(h + 1) * GDN_HEAD_DIM]


def _resident(shape):
    nd = len(shape)
    return pl.BlockSpec(shape, lambda *_: (0,) * nd, pipeline_mode=pl.Buffered(1))


def _params(*sem):
    return pltpu.CompilerParams(dimension_semantics=sem, vmem_limit_bytes=VMEM_LIMIT)


def _swiglu(x, win_ref, wout_ref, acc_ref):
    xb = x.astype(BF16)
    for c in range(D_FF // FF_CHUNK):
        lo = c * FF_CHUNK
        gate = _dot(xb, win_ref[:, lo:lo + FF_CHUNK])
        up = _dot(xb, win_ref[:, D_FF + lo:D_FF + lo + FF_CHUNK])
        h = (gate * jax.nn.sigmoid(gate) * up).astype(BF16)
        part = _dot(h, wout_ref[lo:lo + FF_CHUNK, :])
        if c == 0:
            acc_ref[...] = part
        else:
            acc_ref[...] += part
    return acc_ref[...]


def _ffn_body(x_ref, win_ref, wout_ref, g_ref, b_ref, o_ref, acc_ref):
    x = x_ref[...]
    o_ref[...] = _layer_norm(ALPHA * x + 0.5 * _swiglu(x, win_ref, wout_ref, acc_ref), g_ref[...], b_ref[...])


def _ffn_ln(x, w_in, w_out, g, b):
    n, d = x.shape
    tm = min(ROW_TILE, n)
    row = pl.BlockSpec((tm, d), lambda i: (i, 0))
    return pl.pallas_call(
        _ffn_body,
        grid=(n // tm,),
        in_specs=[row, _resident(w_in.shape), _resident(w_out.shape),
                  _resident(g.shape), _resident(b.shape)],
        out_specs=row,
        out_shape=jax.ShapeDtypeStruct((n, d), F32),
        scratch_shapes=[pltpu.VMEM((tm, d), F32)],
        compiler_params=_params("parallel"),
        name="ffn_ln",
    )(x, w_in, w_out, g, b)


def _causal_conv(hist_ref, w_ref, rows):
    taps = w_ref.shape[0]
    xe = hist_ref[...]
    return sum(w_ref[taps - 1 - d:taps - d, :] * (xe if d == 0 else pltpu.roll(xe, d, 0))[HALO:HALO + rows, :]
               for d in range(taps))


def _proj_body(h_ref, w_ref, gcw_ref, scw_ref, sb_ref, gqkv_ref, gz_ref, ga_ref, gb_ref, osc_ref,
               ghist_ref, shist_ref, *, rows, tiles_per_seq):
    first = lax.rem(pl.program_id(0), tiles_per_seq) == 0

    @pl.when(first)
    def _():
        ghist_ref[0:HALO, :] = jnp.zeros((HALO, 3 * GDN_WIDTH), F32)
        shist_ref[0:HALO, :] = jnp.zeros((HALO, SC_WIDTH), F32)

    @pl.when(jnp.logical_not(first))
    def _():
        ghist_ref[0:HALO, :] = ghist_ref[rows:rows + HALO, :]
        shist_ref[0:HALO, :] = shist_ref[rows:rows + HALO, :]

    hb = h_ref[...].astype(BF16)
    lo = 0

    def project(width):
        nonlocal lo
        out = _dot(hb, w_ref[:, lo:lo + width])
        lo += width
        return out

    sb_ref[...] = project(3 * SB_WIDTH).astype(BF16)
    ghist_ref[HALO:HALO + rows, :] = project(3 * GDN_WIDTH)
    gz_ref[...] = project(GDN_WIDTH)
    ga_ref[...] = project(WIDE)
    gb_ref[...] = project(WIDE)
    sc = project(3 * SC_WIDTH)

    shist_ref[HALO:HALO + rows, :] = sc[:, SC_WIDTH:2 * SC_WIDTH] * sc[:, 2 * SC_WIDTH:]
    osc_ref[...] = (sc[:, 0:SC_WIDTH] * _causal_conv(shist_ref, scw_ref, rows)).astype(BF16)

    conv = _causal_conv(ghist_ref, gcw_ref, rows)
    act = conv * jax.nn.sigmoid(conv)
    for h in range(2 * GDN_HEADS):
        t = _head_cols(act, h)
        n = t * lax.rsqrt(jnp.sum(t * t, axis=-1, keepdims=True) + NORM_EPS)
        if h < GDN_HEADS:
            n = n * GDN_HEAD_DIM ** -0.5
        gqkv_ref[:, h * GDN_HEAD_DIM:(h + 1) * GDN_HEAD_DIM] = n.astype(BF16)
    gqkv_ref[:, 2 * GDN_WIDTH:] = act[:, 2 * GDN_WIDTH:].astype(BF16)


_PROJ_OUTS = ((3 * SB_WIDTH, BF16), (3 * GDN_WIDTH, BF16), (GDN_WIDTH, F32), (WIDE, F32), (WIDE, F32), (SC_WIDTH, BF16))


def _proj(h, w, gdn_conv_w, sc_conv_w, seq):
    n, d = h.shape
    tm = min(ROW_TILE, seq)
    return pl.pallas_call(
        functools.partial(_proj_body, rows=tm, tiles_per_seq=seq // tm),
        grid=(n // tm,),
        in_specs=[pl.BlockSpec((tm, d), lambda i: (i, 0)), _resident(w.shape), _resident(gdn_conv_w.shape),
                  _resident(sc_conv_w.shape)],
        out_specs=[pl.BlockSpec((tm, width), lambda i: (i, 0)) for width, _ in _PROJ_OUTS],
        out_shape=[jax.ShapeDtypeStruct((n, width), dtype) for width, dtype in _PROJ_OUTS],
        scratch_shapes=[pltpu.VMEM((tm + HALO, 3 * GDN_WIDTH), F32), pltpu.VMEM((tm + HALO, SC_WIDTH), F32)],
        compiler_params=_params("arbitrary"),
        name="mix_proj",
    )(h, w, gdn_conv_w, sc_conv_w)


def _proj_weight(w_in):
    rep = lambda cols: jnp.repeat(cols, CHUNK, axis=1)
    return jnp.concatenate(
        [w_in[:, :END_GDN_Z], rep(w_in[:, END_GDN_Z:END_GDN_A]), rep(w_in[:, END_GDN_A:END_GDN_B]),
         w_in[:, END_GDN_B:]], axis=1).astype(BF16)


def _sb_body(q_ref, k_ref, v_ref, o_ref, acc_ref, suf_ref, *, rows, tile):
    i = pl.program_id(2)
    subtiles = rows // tile
    heads = LANES // SB_HEAD_DIM
    win = SB_WINDOW_TILES * tile
    lane = lax.broadcasted_iota(jnp.int32, (tile, LANES), 1)

    @pl.when(i == 0)
    def _():
        j = lax.broadcasted_iota(jnp.int32, (2 * win, win), 0)
        j = jnp.where(j >= win, j - win, j)
        s = lax.broadcasted_iota(jnp.int32, (2 * win, win), 1)
        suf_ref[...] = (j >= s).astype(BF16)

    def scores(qh, start, width):
        return _dot_nt(qh, k_ref[0, pl.ds(start, width), :])

    def tails(z, width, lead):
        log_skip = -_softplus(z)
        seen = None
        if lead is not None:
            seen = (lax.broadcasted_iota(jnp.int32, (tile, width), 1)
                    - lax.broadcasted_iota(jnp.int32, (tile, width), 0)) < lead
            log_skip = jnp.where(seen, log_skip, 0.0)
        hi = log_skip.astype(BF16)
        lo = (log_skip - hi.astype(F32)).astype(BF16)
        if width == win:
            op = suf_ref[...]
        else:
            op = jnp.concatenate([suf_ref[0:width, 0:width], suf_ref[win:win + width, 0:width]], axis=0)
        return _dot(jnp.concatenate([hi, lo], axis=1), op), seen

    def weights(z, tail, seen, carry):
        logit = z + tail if carry is None else z + carry + tail
        if seen is not None:
            logit = jnp.where(seen, logit, -jnp.inf)
        total = tail[:, 0:1]
        return jnp.exp(logit).astype(BF16), total if carry is None else carry + total

    def values(att, start, width):
        return _dot(att, v_ref[0, pl.ds(start, width), :])

    setup = []
    for t in range(subtiles):
        q2 = q_ref[0, t * tile:(t + 1) * tile, :] * SB_HEAD_DIM ** -0.5
        diag = i * subtiles + t
        start = pl.multiple_of(jnp.maximum(diag - (SB_WINDOW_TILES - 1), 0) * tile, tile)
        for h in range(heads):
            in_head = (lane >= h * SB_HEAD_DIM) & (lane < (h + 1) * SB_HEAD_DIM)
            setup.append((jnp.where(in_head, q2, jnp.zeros_like(q2)), diag, start))
    zs = [scores(qh, start, win) for qh, _, start in setup]
    ts = [tails(zs[n], win, diag * tile - start) for n, (_, diag, start) in enumerate(setup)]
    wt = [weights(zs[n], tail, seen, None) for n, (tail, seen) in enumerate(ts)]
    problems = [(qh, diag - SB_WINDOW_TILES, wt[n][1], values(wt[n][0], start, win))
                for n, (qh, diag, start) in enumerate(setup)]

    worst = functools.reduce(jnp.maximum, [carry for _, _, carry, _ in problems])
    for n, (_, _, _, acc) in enumerate(problems):
        acc_ref[n] = acc

    @pl.when(jnp.logical_and(i * subtiles + subtiles - 1 >= SB_WINDOW_TILES, jnp.max(worst) > -SB_SKIP_LOG))
    def _():
        for n, (qh, first, carry, acc) in enumerate(problems):
            def cond(state):
                kb, carry, _ = state
                return jnp.logical_and(kb >= 0, jnp.max(carry) > -SB_SKIP_LOG)

            def body(state, qh=qh):
                kb, carry, acc = state
                start = pl.multiple_of(kb * tile, tile)
                z = scores(qh, start, tile)
                att, carry = weights(z, tails(z, tile, None)[0], None, carry)
                return kb - 1, carry, acc + values(att, start, tile)

            acc_ref[n] = lax.while_loop(cond, body, (first, carry, acc))[2]

    for t in range(subtiles):
        out = acc_ref[t * heads]
        for h in range(1, heads):
            out = jnp.where(lane >= h * SB_HEAD_DIM, acc_ref[t * heads + h], out)
        o_ref[0, t * tile:(t + 1) * tile, :] = out.astype(o_ref.dtype)


def _sb_attention(qkv):
    bsz, s, _ = qkv.shape
    rows = min(SB_ROWS, s)
    tile = min(SB_TILE, rows // 2)
    assert s >= SB_WINDOW_TILES * tile
    pairs = SB_WIDTH // LANES
    full = lambda off: pl.BlockSpec((1, s, LANES), lambda b, p, i: (b, 0, off + p))
    blk = pl.BlockSpec((1, rows, LANES), lambda b, p, i: (b, i, p))
    problems = (rows // tile) * (LANES // SB_HEAD_DIM)
    return pl.pallas_call(
        functools.partial(_sb_body, rows=rows, tile=tile),
        grid=(bsz, pairs, s // rows),
        in_specs=[blk, full(pairs), full(2 * pairs)],
        out_specs=blk,
        out_shape=jax.ShapeDtypeStruct((bsz, s, SB_WIDTH), BF16),
        scratch_shapes=[pltpu.VMEM((problems, tile, LANES), F32),
                        pltpu.VMEM((2 * SB_WINDOW_TILES * tile, SB_WINDOW_TILES * tile), BF16)],
        compiler_params=_params("parallel", "parallel", "arbitrary"),
        name="sb_attention",
    )(qkv, qkv, qkv)


def _split3(x):
    hi = x.astype(BF16)
    r1 = x - hi.astype(F32)
    mid = r1.astype(BF16)
    lo = (r1 - mid.astype(F32)).astype(BF16)
    return hi, mid, lo


def _block_diag(x, width):
    tiled = jnp.concatenate([x] * GDN_HEADS, axis=0)
    r = lax.broadcasted_iota(jnp.int32, tiled.shape, 0) // CHUNK
    l = lax.broadcasted_iota(jnp.int32, tiled.shape, 1) // width
    return jnp.where(r == l, tiled, jnp.zeros_like(tiled))


def _gdn_intra(qs, ks, vs, a_logits, b_logits, a_scale, dt_bias):
    c = CHUNK
    units = range(len(qs))
    row = lax.broadcasted_iota(jnp.int32, (c, WIDE), 0)
    lane = lax.broadcasted_iota(jnp.int32, (c, WIDE), 1)
    j = lane & (c - 1)
    eye = row == j
    incl = row >= j
    strict = row > j

    def to_row(x_col):
        return jnp.sum(jnp.where(eye, x_col, 0.0), axis=0, keepdims=True)

    tri = (lax.broadcasted_iota(jnp.int32, (c, c), 0) >= lax.broadcasted_iota(jnp.int32, (c, c), 1)).astype(BF16)
    g = [a_scale * _softplus(a + dt_bias) for a in a_logits]
    beta = [jax.nn.sigmoid(b) for b in b_logits]
    gcum = [sum(_dot(tri, part) for part in _split3(gi)) for gi in g]
    grow = [to_row(t) for t in gcum]
    brow = [to_row(t) for t in beta]
    decay = [jnp.exp(jnp.where(incl, gcum[n] - grow[n], -jnp.inf)) for n in units]

    k_bd = [_block_diag(k, GDN_HEAD_DIM) for k in ks]
    kk = [_dot_nt(ks[n], k_bd[n]) for n in units]
    qk = [_dot_nt(qs[n], k_bd[n]) for n in units]
    m = [jnp.where(strict, kk[n] * beta[n] * decay[n], 0.0) for n in units]

    x = [jnp.where(eye, 1.0, 0.0) - jnp.where((row ^ j) == 1, mi, 0.0) for mi in m]
    size = 2
    while size < c:
        joined = ((row ^ j) >= size) & ((row ^ j) < 2 * size)
        xb = [xi.astype(BF16) for xi in x]
        y = [_dot(jnp.where(joined, m[n], 0.0).astype(BF16), _block_diag(xb[n], c)) for n in units]
        z = [_dot(xb[n], _block_diag(y[n].astype(BF16), c)) for n in units]
        x = [x[n] - z[n] for n in units]
        size *= 2

    t_u = [x[n] * brow[n] for n in units]
    t_w = [t_u[n] * jnp.exp(grow[n]) for n in units]
    u = [_dot(t_u[n].astype(BF16), _block_diag(vs[n], GDN_HEAD_DIM)) for n in units]
    w = [_dot(t_w[n].astype(BF16), k_bd[n]) for n in units]

    spread = lambda t: jnp.concatenate(
        [jnp.broadcast_to(t[:, h * c:h * c + 1], (c, GDN_HEAD_DIM)) for h in range(GDN_HEADS)], axis=1)
    return [(u[n], w[n].astype(BF16), jnp.where(incl, qk[n] * decay[n], 0.0).astype(BF16),
             spread(jnp.exp(gcum[n])), spread(jnp.exp(gcum[n][c - 1:c, :] - gcum[n]))) for n in units]


def _gdn_body(x_ref, z_ref, a_ref, b_ref, ascale_ref, dtb_ref, nw_ref, o_ref,
              u_ref, w_ref, qk_ref, eg_ref, ed_ref, oraw_ref, s_ref, *, bsz, rows):
    chunks = rows // CHUNK
    heads = range(GDN_HEADS)
    batch = range(bsz)
    qkv_ref = x_ref

    @pl.when(pl.program_id(0) == 0)
    def _():
        s_ref[...] = jnp.zeros_like(s_ref)

    a_scale = ascale_ref[...]
    dt_bias = dtb_ref[...]

    units = [(b, pl.ds(ci * CHUNK, CHUNK)) for ci in range(chunks) for b in batch]
    results = _gdn_intra(
        [qkv_ref[b, r, 0:GDN_WIDTH] for b, r in units], [qkv_ref[b, r, GDN_WIDTH:2 * GDN_WIDTH] for b, r in units],
        [qkv_ref[b, r, 2 * GDN_WIDTH:] for b, r in units], [a_ref[b, r, :] for b, r in units],
        [b_ref[b, r, :] for b, r in units], a_scale, dt_bias)
    for (b, r), (u, w, qk, eg, ed) in zip(units, results):
        u_ref[b, r, :] = u
        w_ref[b, r, :] = w
        qk_ref[b, r, :] = qk
        eg_ref[b, r, :] = eg
        ed_ref[b, r, :] = ed

    def recur(ci, _):
        r = pl.ds(pl.multiple_of(ci * CHUNK, CHUNK), CHUNK)
        last = pl.ds(ci * CHUNK + CHUNK - 1, 1)
        states = [[s_ref[b, h] for h in heads] for b in batch]
        sbf = [[s.astype(BF16) for s in row] for row in states]
        v_new = [[_head_cols(u_ref[b, r, :], h) - _dot(_head_cols(w_ref[b, r, :], h), sbf[b][h]) for h in heads]
                 for b in batch]
        carried = [jnp.concatenate([_dot(_head_cols(qkv_ref[b, r, 0:GDN_WIDTH], h), sbf[b][h]) for h in heads], axis=1)
                   for b in batch]
        local = [_dot(qk_ref[b, r, :], _block_diag(jnp.concatenate(v_new[b], axis=1).astype(BF16), GDN_HEAD_DIM))
                 for b in batch]
        for b in batch:
            k = qkv_ref[b, r, GDN_WIDTH:2 * GDN_WIDTH]
            ed = ed_ref[b, r, :]
            eg_last = eg_ref[b, last, :]
            for h in heads:
                k_dec_v = (v_new[b][h] * _head_cols(ed, h)).astype(BF16)
                s_ref[b, h] = states[b][h] * _head_cols(eg_last, h) + _dot_tn(_head_cols(k, h), k_dec_v)
        for b in batch:
            oraw_ref[b, r, :] = eg_ref[b, r, :] * carried[b] + local[b]
        return 0

    lax.fori_loop(0, chunks, recur, 0, unroll=True)

    norm_w = nw_ref[...]
    for b in batch:
        for h in heads:
            o = _head_cols(oraw_ref[b], h)
            y = o * lax.rsqrt(jnp.mean(o * o, axis=-1, keepdims=True) + NORM_EPS) * norm_w
            zh = _head_cols(z_ref[b], h)
            o_ref[b, :, h * GDN_HEAD_DIM:(h + 1) * GDN_HEAD_DIM] = (y * (zh * jax.nn.sigmoid(zh))).astype(o_ref.dtype)


def _gdn(gqkv, gz, ga, gb, a_log, dt_bias, norm_w):
    bsz, s, _ = gqkv.shape
    rows = min(GDN_ROWS, s)
    a_scale = jnp.repeat(-jnp.exp(a_log.astype(F32)), CHUNK)[None, :]
    dtb = jnp.repeat(dt_bias.astype(F32), CHUNK)[None, :]
    blk = lambda width: pl.BlockSpec((bsz, rows, width), lambda i: (0, i, 0))
    buf = lambda width, dtype: pltpu.VMEM((bsz, rows, width), dtype)
    return pl.pallas_call(
        functools.partial(_gdn_body, bsz=bsz, rows=rows),
        grid=(s // rows,),
        in_specs=[blk(3 * GDN_WIDTH), blk(GDN_WIDTH), blk(WIDE), blk(WIDE),
                  _resident(a_scale.shape), _resident(dtb.shape), _resident((1, GDN_HEAD_DIM))],
        out_specs=blk(GDN_WIDTH),
        out_shape=jax.ShapeDtypeStruct((bsz, s, GDN_WIDTH), BF16),
        scratch_shapes=[buf(GDN_WIDTH, F32), buf(GDN_WIDTH, BF16), buf(WIDE, BF16),
                        buf(GDN_WIDTH, F32), buf(GDN_WIDTH, F32), buf(GDN_WIDTH, F32),
                        pltpu.VMEM((bsz, GDN_HEADS, GDN_HEAD_DIM, GDN_HEAD_DIM), F32)],
        compiler_params=_params("arbitrary"),
        name="gated_deltanet",
    )(gqkv, gz, ga, gb, a_scale, dtb, norm_w[None, :])


def _tail_body(x_ref, osb_ref, ogdn_ref, osc_ref, p_ref, wmix_ref, win_ref, wout_ref, wg_ref, bg_ref, wp_ref,
               g_ref, b_ref, o_ref, acc_ref):
    norm = lambda y, j: _layer_norm(y, g_ref[j:j + 1, :], b_ref[j:j + 1, :])
    mix = (_dot(osb_ref[...], wmix_ref[0:SB_WIDTH, :])
           + _dot(ogdn_ref[...], wmix_ref[SB_WIDTH:SB_WIDTH + GDN_WIDTH, :])
           + _dot(osc_ref[...], wmix_ref[SB_WIDTH + GDN_WIDTH:, :]))
    x1 = norm(ALPHA * x_ref[...] + mix, 1)
    x2 = norm(ALPHA * x1 + 0.5 * _swiglu(x1, win_ref, wout_ref, acc_ref), 2)
    gate = jax.nn.sigmoid(_dot(x2.astype(BF16), wg_ref[...]) + bg_ref[...])
    emb = _dot(p_ref[...].astype(BF16), wp_ref[...])
    o_ref[...] = norm(ALPHA * x2 + gate * emb, 3)


def _layer_tail(x, o_sb, o_gdn, o_sc, p, w_mix, w_in, w_out, w_gate, b_gate, w_proj, g, b):
    n, d = x.shape
    tm = min(ROW_TILE, n)
    row = lambda width: pl.BlockSpec((tm, width), lambda i: (i, 0))
    weights = (w_mix, w_in, w_out, w_gate, b_gate, w_proj, g, b)
    return pl.pallas_call(
        _tail_body,
        grid=(n // tm,),
        in_specs=[row(d), row(SB_WIDTH), row(GDN_WIDTH), row(SC_WIDTH), row(p.shape[1])]
                 + [_resident(w.shape) for w in weights],
        out_specs=row(d),
        out_shape=jax.ShapeDtypeStruct((n, d), F32),
        scratch_shapes=[pltpu.VMEM((tm, d), F32)],
        compiler_params=_params("parallel"),
        name="layer_tail",
    )(x, o_sb, o_gdn, o_sc, p, *weights)


def _token_mix(x, bsz, s, w_in, gdn_conv_w, gdn_a_log, gdn_dt_bias, gdn_norm_w, sc_conv_w):
    sb, gqkv, gz, ga, gb, o_sc = _proj(x, _proj_weight(w_in), gdn_conv_w, sc_conv_w, s)
    shaped = lambda t: t.reshape(bsz, s, t.shape[-1])
    o_sb = _sb_attention(shaped(sb))
    o_gdn = _gdn(shaped(gqkv), shaped(gz), shaped(ga), shaped(gb), gdn_a_log, gdn_dt_bias, gdn_norm_w)
    return o_sb.reshape(bsz * s, -1), o_gdn.reshape(bsz * s, -1), o_sc


def kernel(x, p, ln_g, ln_b, ffn_w_in, ffn_w_out, mix_w_in, gdn_conv_w, gdn_a_log, gdn_dt_bias, gdn_norm_w,
           sc_conv_w, mix_w_out, ple_w_proj, ple_w_gate, ple_b_gate):
    bsz, s, d = x.shape
    n = bsz * s
    x = x.reshape(n, d)
    for i in range(ln_g.shape[0]):
        x = _ffn_ln(x, ffn_w_in[i, 0].astype(BF16), ffn_w_out[i, 0].astype(BF16), ln_g[i, 0][None, :], ln_b[i, 0][None, :])
        o_sb, o_gdn, o_sc = _token_mix(x, bsz, s, mix_w_in[i], gdn_conv_w[i], gdn_a_log[i], gdn_dt_bias[i],
                                       gdn_norm_w[i], sc_conv_w[i])
        x = _layer_tail(x, o_sb, o_gdn, o_sc, p[i].reshape(n, -1), mix_w_out[i].astype(BF16),
                        ffn_w_in[i, 1].astype(BF16), ffn_w_out[i, 1].astype(BF16), ple_w_gate[i].astype(BF16),
                        ple_b_gate[i][None, :], ple_w_proj[i].astype(BF16), ln_g[i], ln_b[i])
    return x.reshape(bsz, s, d)
```
